```python
import jax, jax.numpy as jnp
from jax import lax
import numpy as np

D_MODEL = 4096
BATCH = 2
SEQ = 4096
DEPTH = 4

GRID_W = 64
N_HEADS = 16
HEAD_DIM = 128
ATTN_W = N_HEADS * HEAD_DIM
WIN_R = 8
WIN_C = 16
POOL_WINDOWS = (2, 4, 8, 16)
POOL_GROUP = 256
POOL_W = POOL_GROUP * len(POOL_WINDOWS)
GMLP_CHUNK = 128
GMLP_GROUPS = 4
GMLP_GROUP_W = 256
GMLP_W = GMLP_GROUPS * GMLP_GROUP_W
MIX_W = ATTN_W + POOL_W + GMLP_W
IN_W = 3 * ATTN_W + POOL_W + 2 * GMLP_W
N_BRANCH = 3
GATE_RANK = 512
D_FF = (8 * D_MODEL + 3 * 256 - 1) // (3 * 256) * 256
EPS = 1e-6

kernel_name = "hybrid_natten_pool_gmlp_encoder"


def rms_norm(x, g):
    xf = x.astype(jnp.float32)
    y = xf * lax.rsqrt(jnp.mean(xf * xf, axis=-1, keepdims=True) + EPS)
    return (y * g.astype(jnp.float32)).astype(x.dtype)


def layer_norm(x, g, b):
    xf = x.astype(jnp.float32)
    mu = jnp.mean(xf, axis=-1, keepdims=True)
    xc = xf - mu
    y = xc * lax.rsqrt(jnp.mean(xc * xc, axis=-1, keepdims=True) + EPS)
    return (y * g.astype(jnp.float32) + b.astype(jnp.float32)).astype(x.dtype)


def neighbourhood_attention(q, k, v, rpb):
    B, T, H, hd = q.shape
    rows = T // GRID_W
    wr = min(WIN_R, rows)
    r = jnp.arange(rows)
    row_start = jnp.clip(r - wr // 2, 0, rows - wr)
    row_idx = row_start[:, None] + jnp.arange(wr)[None, :]
    c = jnp.arange(GRID_W)
    col_start = jnp.clip(c - WIN_C // 2, 0, GRID_W - WIN_C)
    col_valid = (c[None, :] >= col_start[:, None]) & (c[None, :] < col_start[:, None] + WIN_C)
    ro = row_idx - r[:, None] + (WIN_R - 1)
    co = jnp.clip(c[None, :] - c[:, None], -(WIN_C - 1), WIN_C - 1) + (WIN_C - 1)
    bias = rpb[:, ro[:, None, :, None], co[None, :, None, :]]
    qg = q.reshape(B, rows, GRID_W, H, hd)
    kg = k.reshape(B, rows, GRID_W, H, hd)[:, row_idx]
    vg = v.reshape(B, rows, GRID_W, H, hd)[:, row_idx]
    s = jnp.einsum('brqhd,brwkhd->bhrqwk', qg, kg).astype(jnp.float32) * (HEAD_DIM ** -0.5)
    s = s + bias.astype(jnp.float32)[None]
    s = jnp.where(col_valid[:, None, :], s, -jnp.inf)
    p = jax.nn.softmax(s.reshape(B, H, rows, GRID_W, wr * GRID_W), axis=-1)
    p = p.reshape(s.shape).astype(v.dtype)
    o = jnp.einsum('bhrqwk,brwkhd->brqhd', p, vg)
    return o.reshape(B, T, H * hd)


def multiscale_pool(p, w_grp, scale):
    T = p.shape[1]
    pf = p.astype(jnp.float32)
    cs = jnp.concatenate([jnp.zeros_like(pf[:, :1]), jnp.cumsum(pf, axis=1)], axis=1)
    t = jnp.arange(T)
    outs = []
    for g, w in enumerate(POOL_WINDOWS):
        sl = slice(g * POOL_GROUP, (g + 1) * POOL_GROUP)
        lo = jnp.clip(t - w // 2, 0, T - 1)
        hi = jnp.clip(t - w // 2 + w - 1, 0, T - 1)
        csg = cs[..., sl]
        cnt = (hi - lo + 1).astype(jnp.float32)[None, :, None]
        mean = (jnp.take(csg, hi + 1, axis=1) - jnp.take(csg, lo, axis=1)) / cnt
        d = (mean - pf[..., sl]).astype(p.dtype)
        outs.append(d @ w_grp[g])
    return jnp.concatenate(outs, axis=-1) * scale


def spatial_gating(uv, ln_g, ln_b, w_s, b_s):
    uv = jax.nn.gelu(uv)
    u, v = uv[..., :GMLP_W], uv[..., GMLP_W:]
    v = layer_norm(v, ln_g, ln_b)
    B, T, _ = v.shape
    nc = T // GMLP_CHUNK
    vc = v.reshape(B, nc, GMLP_CHUNK, GMLP_GROUPS, GMLP_GROUP_W)
    mixed = jnp.einsum('gij,bcjgd->bcigd', w_s, vc) + b_s.T[None, None, :, :, None]
    return u * mixed.reshape(B, T, GMLP_W)


def setup_inputs(seed: int = 0) -> dict:
    key = jax.random.key(seed)
    ks = jax.random.split(key, 20)
    L, D = DEPTH, D_MODEL
    nrm = jax.random.normal
    return {
        "x": nrm(ks[0], (BATCH, SEQ, D), jnp.float32),
        "attn_norm_g": 1.0 + 0.02 * nrm(ks[1], (L, D), jnp.float32),
        "w_in": nrm(ks[2], (L, D, IN_W), jnp.float32) * D ** -0.5,
        "rpb": 0.1 * nrm(ks[3], (L, N_HEADS, 2 * WIN_R - 1, 2 * WIN_C - 1), jnp.float32),
        "pool_w": nrm(ks[4], (L, len(POOL_WINDOWS), POOL_GROUP, POOL_GROUP), jnp.float32) * POOL_GROUP ** -0.5,
        "pool_scale": 1.0 + 0.02 * nrm(ks[5], (L, POOL_W), jnp.float32),
        "gmlp_ln_g": 1.0 + 0.02 * nrm(ks[6], (L, GMLP_W), jnp.float32),
        "gmlp_ln_b": 0.02 * nrm(ks[7], (L, GMLP_W), jnp.float32),
        "gmlp_w_s": nrm(ks[8], (L, GMLP_GROUPS, GMLP_CHUNK, GMLP_CHUNK), jnp.float32) * GMLP_CHUNK ** -0.5,
        "gmlp_b_s": 0.02 * nrm(ks[9], (L, GMLP_GROUPS, GMLP_CHUNK), jnp.float32),
        "w_branch": nrm(ks[10], (L, MIX_W, D), jnp.float32) * (MIX_W // 4) ** -0.5,
        "gate_down": nrm(ks[11], (L, D, GATE_RANK), jnp.float32) * D ** -0.5,
        "gate_up": nrm(ks[12], (L, GATE_RANK, N_BRANCH * D), jnp.float32) * GATE_RANK ** -0.5,
        "gate_b": 0.02 * nrm(ks[13], (L, N_BRANCH * D), jnp.float32),
        "w_out": nrm(ks[14], (L, D, D), jnp.float32) * D ** -0.5,
        "ffn_norm_g": 1.0 + 0.02 * nrm(ks[15], (L, D), jnp.float32),
        "w_ffn_gate": nrm(ks[16], (L, D, D_FF), jnp.float32) * D ** -0.5,
        "w_ffn_up": nrm(ks[17], (L, D, D_FF), jnp.float32) * D ** -0.5,
        "w_ffn_down": nrm(ks[18], (L, D_FF, D), jnp.float32) * D_FF ** -0.5,
        "final_norm_g": 1.0 + 0.02 * nrm(ks[19], (D,), jnp.float32),
    }


def reference(x, attn_norm_g, w_in, rpb, pool_w, pool_scale, gmlp_ln_g, gmlp_ln_b, gmlp_w_s,
              gmlp_b_s, w_branch, gate_down, gate_up, gate_b, w_out, ffn_norm_g, w_ffn_gate,
              w_ffn_up, w_ffn_down, final_norm_g):
    B, T, D = x.shape
    o_k = ATTN_W
    o_v = 2 * ATTN_W
    o_p = 3 * ATTN_W
    o_g = 3 * ATTN_W + POOL_W
    for l in range(DEPTH):
        h = rms_norm(x, attn_norm_g[l])
        z = h @ w_in[l]
        q = z[..., :o_k].reshape(B, T, N_HEADS, HEAD_DIM)
        k = z[..., o_k:o_v].reshape(B, T, N_HEADS, HEAD_DIM)
        v = z[..., o_v:o_p].reshape(B, T, N_HEADS, HEAD_DIM)
        y_attn = neighbourhood_attention(q, k, v, rpb[l])
        y_pool = multiscale_pool(z[..., o_p:o_g], pool_w[l], pool_scale[l])
        y_sg = spatial_gating(z[..., o_g:], gmlp_ln_g[l], gmlp_ln_b[l], gmlp_w_s[l], gmlp_b_s[l])
        wb = w_branch[l]
        b_attn = y_attn @ wb[:ATTN_W]
        b_pool = y_pool @ wb[ATTN_W:ATTN_W + POOL_W]
        b_sg = y_sg @ wb[ATTN_W + POOL_W:]
        gates = jax.nn.sigmoid((h @ gate_down[l]) @ gate_up[l] + gate_b[l]).reshape(B, T, N_BRANCH, D)
        merged = gates[:, :, 0] * b_attn + gates[:, :, 1] * b_pool + gates[:, :, 2] * b_sg
        x = x + merged @ w_out[l]
        h = rms_norm(x, ffn_norm_g[l])
        x = x + (jax.nn.silu(h @ w_ffn_gate[l]) * (h @ w_ffn_up[l])) @ w_ffn_down[l]
    return rms_norm(x, final_norm_g)
```

```python
import functools

import jax
import jax.numpy as jnp
from jax import lax
from jax.experimental import pallas as pl
from jax.experimental.pallas import tpu as pltpu

F32 = jnp.float32
BF16 = jnp.bfloat16

GRID_W = 64
N_HEADS = 16
HEAD_DIM = 128
ATTN_W = N_HEADS * HEAD_DIM
WIN_R = 8
WIN_C = 16
POOL_WINDOWS = (2, 4, 8, 16)
POOL_GROUP = 256
POOL_W = POOL_GROUP * len(POOL_WINDOWS)
GMLP_CHUNK = 128
GMLP_GROUPS = 4
GMLP_GROUP_W = 256
GMLP_W = GMLP_GROUPS * GMLP_GROUP_W
N_BRANCH = 3
EPS = 1e-6

VMEM_LIMIT_BYTES = 56 * 1024 * 1024

ATTN_Q_ROWS = 4
ATTN_K_ROWS = ATTN_Q_ROWS + WIN_R - 1
POOL_CHUNK = 256
POOL_HALO = 128
MASKED = -1e30


def _params(*sem):
    return pltpu.CompilerParams(dimension_semantics=sem, vmem_limit_bytes=VMEM_LIMIT_BYTES)


def _rmsnorm_kernel(x_ref, g_ref, o_ref):
    x = x_ref[...]
    ms = jnp.mean(x * x, axis=-1, keepdims=True)
    o_ref[...] = (x * lax.rsqrt(ms + EPS) * g_ref[...]).astype(o_ref.dtype)


def _rmsnorm(x, g, out_dtype, tm=256):
    m, d = x.shape
    return pl.pallas_call(
        _rmsnorm_kernel,
        grid=(m // tm,),
        in_specs=[pl.BlockSpec((tm, d), lambda i: (i, 0)),
                  pl.BlockSpec((1, d), lambda i: (0, 0))],
        out_specs=pl.BlockSpec((tm, d), lambda i: (i, 0)),
        out_shape=jax.ShapeDtypeStruct((m, d), out_dtype),
        compiler_params=_params("parallel"),
        name="rmsnorm",
    )(x, g.reshape(1, d))


def _mm_kernel(a_ref, w_ref, o_ref):
    o_ref[...] = jnp.dot(a_ref[...], w_ref[...], preferred_element_type=F32).astype(o_ref.dtype)


def _mm_res_kernel(a_ref, w_ref, r_ref, o_ref):
    o_ref[...] = r_ref[...] + jnp.dot(a_ref[...], w_ref[...], preferred_element_type=F32)


def _matmul(a, w, *, tm, tn, out_dtype, residual=None, name="matmul"):
    m, k = a.shape
    n = w.shape[1]
    in_specs = [pl.BlockSpec((tm, k), lambda i, j: (i, 0)),
                pl.BlockSpec((k, tn), lambda i, j: (0, j))]
    args = [a, w]
    kern = _mm_kernel
    if residual is not None:
        in_specs.append(pl.BlockSpec((tm, tn), lambda i, j: (i, j)))
        args.append(residual)
        kern = _mm_res_kernel
    return pl.pallas_call(
        kern,
        grid=(m // tm, pl.cdiv(n, tn)),
        in_specs=in_specs,
        out_specs=pl.BlockSpec((tm, tn), lambda i, j: (i, j)),
        out_shape=jax.ShapeDtypeStruct((m, n), out_dtype),
        compiler_params=_params("parallel", "arbitrary"),
        name=name,
    )(*args)


def _swiglu_kernel(h_ref, wg_ref, wu_ref, o_ref):
    h = h_ref[...]
    a = jnp.dot(h, wg_ref[...], preferred_element_type=F32)
    b = jnp.dot(h, wu_ref[...], preferred_element_type=F32)
    o_ref[...] = (a * (1.0 / (1.0 + jnp.exp(-a))) * b).astype(o_ref.dtype)


def _swiglu(h, wg, wu, *, tm, tn):
    m, k = h.shape
    n = wg.shape[1]
    return pl.pallas_call(
        _swiglu_kernel,
        grid=(m // tm, pl.cdiv(n, tn)),
        in_specs=[pl.BlockSpec((tm, k), lambda i, j: (i, 0)),
                  pl.BlockSpec((k, tn), lambda i, j: (0, j)),
                  pl.BlockSpec((k, tn), lambda i, j: (0, j))],
        out_specs=pl.BlockSpec((tm, tn), lambda i, j: (i, j)),
        out_shape=jax.ShapeDtypeStruct((m, n), BF16),
        compiler_params=_params("parallel", "arbitrary"),
        name="swiglu",
    )(h, wg, wu)


def _attn_group_start(g, rows):
    return jnp.clip(g * ATTN_Q_ROWS - WIN_R // 2, 0, rows - ATTN_K_ROWS)


def _attn_bias_tables(rpb, rows):
    n_groups = rows // ATTN_Q_ROWS
    tables = []
    for g in (0, 1, n_groups - 1):
        r0 = g * ATTN_Q_ROWS
        start = int(min(max(r0 - WIN_R // 2, 0), rows - ATTN_K_ROWS))
        qr = (r0 + jnp.arange(ATTN_Q_ROWS))[:, None, None, None]
        qc = jnp.arange(GRID_W)[None, :, None, None]
        kr = (start + jnp.arange(ATTN_K_ROWS))[None, None, :, None]
        kc = jnp.arange(GRID_W)[None, None, None, :]
        rs = jnp.clip(qr - WIN_R // 2, 0, rows - WIN_R)
        cs = jnp.clip(qc - WIN_C // 2, 0, GRID_W - WIN_C)
        valid = (kr >= rs) & (kr < rs + WIN_R) & (kc >= cs) & (kc < cs + WIN_C)
        ro = jnp.clip(kr - qr + (WIN_R - 1), 0, 2 * WIN_R - 2)
        co = jnp.clip(kc - qc, -(WIN_C - 1), WIN_C - 1) + (WIN_C - 1)
        ro, co, valid = jnp.broadcast_arrays(ro, co, valid)
        bias = rpb[:, ro, co].astype(F32)
        t = jnp.where(valid[None], bias, MASKED)
        tables.append(t.reshape(rpb.shape[0], ATTN_Q_ROWS * GRID_W, ATTN_K_ROWS * GRID_W))
    return jnp.stack(tables)


def _attn_kernel(q_ref, k_ref, v_ref, t_ref, o_ref, *, rows):
    g = pl.program_id(2)
    start = pl.multiple_of(_attn_group_start(g, rows) * GRID_W, GRID_W)
    nk = ATTN_K_ROWS * GRID_W
    kw = k_ref[pl.ds(start, nk), :]
    vw = v_ref[pl.ds(start, nk), :]
    s = lax.dot_general(q_ref[...], kw, (((1,), (1,)), ((), ())), preferred_element_type=F32)
    s = s * (HEAD_DIM ** -0.5) + t_ref[0, 0]
    p = jnp.exp(s - jnp.max(s, axis=-1, keepdims=True))
    l = jnp.sum(p, axis=-1, keepdims=True)
    o = jnp.dot(p.astype(BF16), vw, preferred_element_type=F32)
    o_ref[...] = (o / l).astype(o_ref.dtype)


def _attention(z, tables, batch, seq):
    rows = seq // GRID_W
    n_groups = rows // ATTN_Q_ROWS
    tq = ATTN_Q_ROWS * GRID_W
    tk = ATTN_K_ROWS * GRID_W

    def table_type(g):
        return jnp.where(g == 0, 0, jnp.where(g == n_groups - 1, 2, 1))

    return pl.pallas_call(
        functools.partial(_attn_kernel, rows=rows),
        grid=(batch, N_HEADS, n_groups),
        in_specs=[pl.BlockSpec((tq, HEAD_DIM), lambda b, h, g: (b * n_groups + g, h)),
                  pl.BlockSpec((seq, HEAD_DIM), lambda b, h, g: (b, N_HEADS + h)),
                  pl.BlockSpec((seq, HEAD_DIM), lambda b, h, g: (b, 2 * N_HEADS + h)),
                  pl.BlockSpec((1, 1, tq, tk), lambda b, h, g: (table_type(g), h, 0, 0))],
        out_specs=pl.BlockSpec((tq, HEAD_DIM), lambda b, h, g: (b * n_groups + g, h)),
        out_shape=jax.ShapeDtypeStruct((batch * seq, ATTN_W), BF16),
        compiler_params=_params("parallel", "parallel", "arbitrary"),
        name="nbr_attention",
    )(z, z, z, tables)


def _pool_band_matrices():
    i = jnp.arange(POOL_CHUNK)[:, None]
    off = jnp.arange(POOL_CHUNK + 2 * POOL_HALO)[None, :] - POOL_HALO - i
    return jnp.stack([((off >= -(w // 2)) & (off <= w // 2 - 1)).astype(BF16) for w in POOL_WINDOWS])


def _pool_kernel(p_ref, band_ref, w_ref, sc_ref, o_ref, pad_ref, *, seq):
    g = pl.program_id(1)
    half = jnp.left_shift(1, g)
    cols = p_ref.shape[1]
    pad_ref[pl.ds(0, POOL_HALO), :] = jnp.zeros((POOL_HALO, cols), BF16)
    pad_ref[pl.ds(POOL_HALO + seq, POOL_HALO), :] = jnp.zeros((POOL_HALO, cols), BF16)
    pad_ref[pl.ds(POOL_HALO, seq), :] = p_ref[...]

    def chunk(c, carry):
        base = pl.multiple_of(c * POOL_CHUNK, POOL_CHUNK)
        ph = pad_ref[pl.ds(base, POOL_CHUNK + 2 * POOL_HALO), :]
        wsum = jnp.dot(band_ref[0], ph, preferred_element_type=F32)
        t = base + lax.broadcasted_iota(jnp.int32, (POOL_CHUNK, 1), 0)
        lo = jnp.maximum(t - half, 0)
        hi = jnp.minimum(t + half - 1, seq - 1)
        cnt = (hi - lo + 1).astype(F32)
        centre = ph[POOL_HALO:POOL_HALO + POOL_CHUNK].astype(F32)
        d = wsum / cnt - centre
        y = jnp.dot(d.astype(BF16), w_ref[0], preferred_element_type=F32) * sc_ref[...]
        o_ref[pl.ds(base, POOL_CHUNK), :] = y.astype(o_ref.dtype)
        return carry

    lax.fori_loop(0, seq // POOL_CHUNK, chunk, 0)


def _pool(z, bands, w_grp, scale, batch, seq, col_block0):
    n_g = len(POOL_WINDOWS)
    return pl.pallas_call(
        functools.partial(_pool_kernel, seq=seq),
        grid=(batch, n_g),
        in_specs=[pl.BlockSpec((seq, POOL_GROUP), lambda b, g: (b, col_block0 + g)),
                  pl.BlockSpec((1, POOL_CHUNK, POOL_CHUNK + 2 * POOL_HALO), lambda b, g: (g, 0, 0)),
                  pl.BlockSpec((1, POOL_GROUP, POOL_GROUP), lambda b, g: (g, 0, 0)),
                  pl.BlockSpec((1, POOL_GROUP), lambda b, g: (0, g))],
        out_specs=pl.BlockSpec((seq, POOL_GROUP), lambda b, g: (b, g)),
        out_shape=jax.ShapeDtypeStruct((batch * seq, POOL_W), BF16),
        scratch_shapes=[pltpu.VMEM((seq + 2 * POOL_HALO, POOL_GROUP), BF16)],
        compiler_params=_params("parallel", "arbitrary"),
        name="multiscale_pool",
    )(z, bands, w_grp, scale.reshape(1, POOL_W))


def _gelu_tanh(x):
    return 0.5 * x * (1.0 + jnp.tanh(0.7978845608028654 * (x + 0.044715 * (x * x * x))))


def _gmlp_kernel(u_ref, v_ref, lng_ref, lnb_ref, ws_ref, bs_ref, o_ref):
    v = _gelu_tanh(v_ref[...].astype(F32))
    mu = jnp.mean(v, axis=-1, keepdims=True)
    vc = v - mu
    var = jnp.mean(vc * vc, axis=-1, keepdims=True)
    vn = (vc * lax.rsqrt(var + EPS) * lng_ref[...] + lnb_ref[...]).astype(BF16)
    n_chunks = u_ref.shape[0] // GMLP_CHUNK
    for c in range(n_chunks):
        r = slice(c * GMLP_CHUNK, (c + 1) * GMLP_CHUNK)
        for g in range(GMLP_GROUPS):
            cs = slice(g * GMLP_GROUP_W, (g + 1) * GMLP_GROUP_W)
            mixed = jnp.dot(ws_ref[g], vn[r, cs], preferred_element_type=F32) + bs_ref[g]
            u = _gelu_tanh(u_ref[r, cs].astype(F32))
            o_ref[r, cs] = (u * mixed).astype(o_ref.dtype)


def _gmlp(z, ln_g, ln_b, w_s, b_s, u_block, tm=512):
    m = z.shape[0]
    return pl.pallas_call(
        _gmlp_kernel,
        grid=(m // tm,),
        in_specs=[pl.BlockSpec((tm, GMLP_W), lambda i: (i, u_block)),
                  pl.BlockSpec((tm, GMLP_W), lambda i: (i, u_block + 1)),
                  pl.BlockSpec((1, GMLP_W), lambda i: (0, 0)),
                  pl.BlockSpec((1, GMLP_W), lambda i: (0, 0)),
                  pl.BlockSpec((GMLP_GROUPS, GMLP_CHUNK, GMLP_CHUNK), lambda i: (0, 0, 0)),
                  pl.BlockSpec((GMLP_GROUPS, GMLP_CHUNK, 1), lambda i: (0, 0, 0))],
        out_specs=pl.BlockSpec((tm, GMLP_W), lambda i: (i, 0)),
        out_shape=jax.ShapeDtypeStruct((m, GMLP_W), BF16),
        compiler_params=_params("parallel"),
        name="spatial_gating",
    )(z, z, ln_g.reshape(1, GMLP_W), ln_b.reshape(1, GMLP_W), w_s, b_s.reshape(GMLP_GROUPS, GMLP_CHUNK, 1))


def _sigmoid(x):
    return 1.0 / (1.0 + jnp.exp(-x))


def _merge_kernel(ya_ref, yp_ref, ys_ref, hd_ref, wa_ref, wp_ref, ws_ref,
                  gu0_ref, gu1_ref, gu2_ref, gb0_ref, gb1_ref, gb2_ref, o_ref):
    hd = hd_ref[...]
    ga = _sigmoid(jnp.dot(hd, gu0_ref[...], preferred_element_type=F32) + gb0_ref[...])
    acc = ga * jnp.dot(ya_ref[...], wa_ref[...], preferred_element_type=F32)
    gp = _sigmoid(jnp.dot(hd, gu1_ref[...], preferred_element_type=F32) + gb1_ref[...])
    acc = acc + gp * jnp.dot(yp_ref[...], wp_ref[...], preferred_element_type=F32)
    gs = _sigmoid(jnp.dot(hd, gu2_ref[...], preferred_element_type=F32) + gb2_ref[...])
    acc = acc + gs * jnp.dot(ys_ref[...], ws_ref[...], preferred_element_type=F32)
    o_ref[...] = acc.astype(o_ref.dtype)


def _merge(ya, yp, ys, hd, wb, gate_up, gate_b, *, tm, tn):
    m = ya.shape[0]
    d = wb.shape[1]
    rank = hd.shape[1]
    nb = d // tn
    pool_blk = ATTN_W // POOL_W
    sg_blk = (ATTN_W + POOL_W) // GMLP_W
    row = lambda w: pl.BlockSpec((tm, w), lambda i, j: (i, 0))
    gate_b = gate_b.reshape(1, N_BRANCH * d)
    return pl.pallas_call(
        _merge_kernel,
        grid=(m // tm, nb),
        in_specs=[row(ATTN_W), row(POOL_W), row(GMLP_W), row(rank),
                  pl.BlockSpec((ATTN_W, tn), lambda i, j: (0, j)),
                  pl.BlockSpec((POOL_W, tn), lambda i, j: (pool_blk, j)),
                  pl.BlockSpec((GMLP_W, tn), lambda i, j: (sg_blk, j)),
                  pl.BlockSpec((rank, tn), lambda i, j: (0, j)),
                  pl.BlockSpec((rank, tn), lambda i, j: (0, nb + j)),
                  pl.BlockSpec((rank, tn), lambda i, j: (0, 2 * nb + j)),
                  pl.BlockSpec((1, tn), lambda i, j: (0, j)),
                  pl.BlockSpec((1, tn), lambda i, j: (0, nb + j)),
                  pl.BlockSpec((1, tn), lambda i, j: (0, 2 * nb + j))],
        out_specs=pl.BlockSpec((tm, tn), lambda i, j: (i, j)),
        out_shape=jax.ShapeDtypeStruct((m, d), BF16),
        compiler_params=_params("parallel", "arbitrary"),
        name="branch_merge",
    )(ya, yp, ys, hd, wb, wb, wb, gate_up, gate_up, gate_up, gate_b, gate_b, gate_b)


def kernel(x, attn_norm_g, w_in, rpb, pool_w, pool_scale, gmlp_ln_g, gmlp_ln_b, gmlp_w_s, gmlp_b_s,
           w_branch, gate_down, gate_up, gate_b, w_out, ffn_norm_g, w_ffn_gate, w_ffn_up, w_ffn_down,
           final_norm_g):
    batch, seq, d = x.shape
    depth = w_in.shape[0]
    rows = seq // GRID_W
    xf = x.reshape(batch * seq, d)
    bands = _pool_band_matrices()
    pool_col_block = (3 * ATTN_W) // POOL_GROUP
    u_block = (3 * ATTN_W + POOL_W) // GMLP_W
    for l in range(depth):
        h = _rmsnorm(xf, attn_norm_g[l], BF16)
        z = _matmul(h, w_in[l].astype(BF16), tm=1024, tn=1024, out_dtype=BF16, name="in_proj")
        hd = _matmul(h, gate_down[l].astype(BF16), tm=1024, tn=gate_down.shape[2], out_dtype=BF16,
                     name="gate_down")
        y_attn = _attention(z, _attn_bias_tables(rpb[l], rows), batch, seq)
        y_pool = _pool(z, bands, pool_w[l].astype(BF16), pool_scale[l], batch, seq, pool_col_block)
        y_sg = _gmlp(z, gmlp_ln_g[l], gmlp_ln_b[l], gmlp_w_s[l].astype(BF16), gmlp_b_s[l], u_block)
        merged = _merge(y_attn, y_pool, y_sg, hd, w_branch[l].astype(BF16), gate_up[l].astype(BF16),
                        gate_b[l], tm=1024, tn=512)
        xf = _matmul(merged, w_out[l].astype(BF16), tm=1024, tn=512, out_dtype=F32, residual=xf,
                     name="out_proj")
        h = _rmsnorm(xf, ffn_norm_g[l], BF16)
        act = _swiglu(h, w_ffn_gate[l].astype(BF16), w_ffn_up[l].astype(BF16), tm=1024, tn=256)
        xf = _matmul(act, w_ffn_down[l].astype(BF16), tm=512, tn=256, out_dtype=F32, residual=xf,
                     name="ffn_down")
    out = _rmsnorm(xf, final_norm_g, F32)
    return out.reshape(batch, seq, d)
```

```python
import functools

import jax
import jax.numpy as jnp
from jax import lax
from jax.experimental import pallas as pl
from jax.experimental.pallas import tpu as pltpu

F32 = jnp.float32
BF16 = jnp.bfloat16

GRID_W = 64
N_HEADS = 16
HEAD_DIM = 128
ATTN_W = N_HEADS * HEAD_DIM
WIN_R = 8
WIN_C = 16
POOL_WINDOWS = (2, 4, 8, 16)
POOL_GROUP = 256
POOL_W = POOL_GROUP * len(POOL_WINDOWS)
GMLP_CHUNK = 128
GMLP_GROUPS = 4
GMLP_GROUP_W = 256
GMLP_W = GMLP_GROUPS * GMLP_GROUP_W
N_BRANCH = 3
EPS = 1e-6

VMEM_LIMIT_BYTES = 56 * 1024 * 1024

ATTN_Q_ROWS = 4
ATTN_K_ROWS = ATTN_Q_ROWS + WIN_R - 1
POOL_CHUNK = 256
POOL_HALO = 128
MASKED = -1e30


def _params(*sem):
    return pltpu.CompilerParams(dimension_semantics=sem, vmem_limit_bytes=VMEM_LIMIT_BYTES)


def _rmsnorm_kernel(x_ref, g_ref, o_ref):
    x = x_ref[...]
    ms = jnp.mean(x * x, axis=-1, keepdims=True)
    o_ref[...] = (x * lax.rsqrt(ms + EPS) * g_ref[...]).astype(o_ref.dtype)


def _rmsnorm(x, g, out_dtype, tm=256):
    m, d = x.shape
    return pl.pallas_call(
        _rmsnorm_kernel,
        grid=(m // tm,),
        in_specs=[pl.BlockSpec((tm, d), lambda i: (i, 0)),
                  pl.BlockSpec((1, d), lambda i: (0, 0))],
        out_specs=pl.BlockSpec((tm, d), lambda i: (i, 0)),
        out_shape=jax.ShapeDtypeStruct((m, d), out_dtype),
        compiler_params=_params("parallel"),
        name="rmsnorm",
    )(x, g.reshape(1, d))


def _mm_res_kernel(a_ref, w_ref, r_ref, o_ref):
    o_ref[...] = r_ref[...] + jnp.dot(a_ref[...], w_ref[...], preferred_element_type=F32)


def _matmul_res(a, w, residual, *, tm, tn, name):
    m, k = a.shape
    n = w.shape[1]
    return pl.pallas_call(
        _mm_res_kernel,
        grid=(m // tm, n // tn),
        in_specs=[pl.BlockSpec((tm, k), lambda i, j: (i, 0)),
                  pl.BlockSpec((k, tn), lambda i, j: (0, j)),
                  pl.BlockSpec((tm, tn), lambda i, j: (i, j))],
        out_specs=pl.BlockSpec((tm, tn), lambda i, j: (i, j)),
        out_shape=jax.ShapeDtypeStruct((m, n), F32),
        compiler_params=_params("parallel", "arbitrary"),
        name=name,
    )(a, w, residual)


def _cast_on_first_m_step(pairs):
    @pl.when(pl.program_id(1) == 0)
    def _():
        for src, dst in pairs:
            dst[...] = src[...].astype(BF16)


def _ws_mm_kernel(a_ref, w_ref, o_ref, wb_ref):
    _cast_on_first_m_step([(w_ref, wb_ref)])
    o_ref[...] = jnp.dot(a_ref[...], wb_ref[...], preferred_element_type=F32).astype(o_ref.dtype)


def _ws_mm_res_kernel(a_ref, w_ref, r_ref, o_ref, wb_ref):
    _cast_on_first_m_step([(w_ref, wb_ref)])
    o_ref[...] = r_ref[...] + jnp.dot(a_ref[...], wb_ref[...], preferred_element_type=F32)


def _matmul_ws(a, w_stack, layer, *, tm, tn, out_dtype, residual=None, name):
    m, k = a.shape
    n = w_stack.shape[2]
    in_specs = [pl.BlockSpec((tm, k), lambda j, i: (i, 0)),
                pl.BlockSpec((None, k, tn), lambda j, i: (layer, 0, j))]
    args = [a, w_stack]
    kern = _ws_mm_kernel
    if residual is not None:
        in_specs.append(pl.BlockSpec((tm, tn), lambda j, i: (i, j)))
        args.append(residual)
        kern = _ws_mm_res_kernel
    return pl.pallas_call(
        kern,
        grid=(n // tn, m // tm),
        in_specs=in_specs,
        out_specs=pl.BlockSpec((tm, tn), lambda j, i: (i, j)),
        out_shape=jax.ShapeDtypeStruct((m, n), out_dtype),
        scratch_shapes=[pltpu.VMEM((k, tn), BF16)],
        compiler_params=_params("arbitrary", "arbitrary"),
        name=name,
    )(*args)


def _swiglu_kernel(h_ref, wg_ref, wu_ref, o_ref, wgb_ref, wub_ref):
    _cast_on_first_m_step([(wg_ref, wgb_ref), (wu_ref, wub_ref)])
    h = h_ref[...]
    a = jnp.dot(h, wgb_ref[...], preferred_element_type=F32)
    b = jnp.dot(h, wub_ref[...], preferred_element_type=F32)
    o_ref[...] = (a * (1.0 / (1.0 + jnp.exp(-a))) * b).astype(o_ref.dtype)


def _swiglu(h, wg_stack, wu_stack, layer, *, tm, tn):
    m, k = h.shape
    n = wg_stack.shape[2]
    w_spec = pl.BlockSpec((None, k, tn), lambda j, i: (layer, 0, j))
    return pl.pallas_call(
        _swiglu_kernel,
        grid=(n // tn, m // tm),
        in_specs=[pl.BlockSpec((tm, k), lambda j, i: (i, 0)), w_spec, w_spec],
        out_specs=pl.BlockSpec((tm, tn), lambda j, i: (i, j)),
        out_shape=jax.ShapeDtypeStruct((m, n), BF16),
        scratch_shapes=[pltpu.VMEM((k, tn), BF16), pltpu.VMEM((k, tn), BF16)],
        compiler_params=_params("arbitrary", "arbitrary"),
        name="swiglu",
    )(h, wg_stack, wu_stack)


def _attn_group_start(g, rows):
    return jnp.clip(g * ATTN_Q_ROWS - WIN_R // 2, 0, rows - ATTN_K_ROWS)


def _attn_bias_tables(rpb, rows):
    h, n_ro, _ = rpb.shape
    n_groups = rows // ATTN_Q_ROWS
    rpb = rpb.astype(F32)
    edge = GRID_W - WIN_C
    ext = jnp.concatenate([jnp.broadcast_to(rpb[:, :, :1], (h, n_ro, edge)), rpb,
                           jnp.broadcast_to(rpb[:, :, -1:], (h, n_ro, edge)),
                           jnp.zeros((h, n_ro, 1), F32)], axis=-1)
    toep = jnp.tile(ext, (1, 1, GRID_W))[:, :, :GRID_W * (2 * GRID_W - 1)]
    toep = toep.reshape(h, n_ro, GRID_W, 2 * GRID_W - 1)[:, :, :, GRID_W - 1:]
    qc = jnp.arange(GRID_W)[:, None]
    kc = jnp.arange(GRID_W)[None, :]
    cs = jnp.clip(qc - WIN_C // 2, 0, GRID_W - WIN_C)
    toep = jnp.where((kc >= cs) & (kc < cs + WIN_C), toep, MASKED)
    by_q = toep.transpose(0, 2, 1, 3).reshape(h, GRID_W, n_ro * GRID_W)
    tables = []
    for g in (0, 1, n_groups - 1):
        r0 = g * ATTN_Q_ROWS
        start = min(max(r0 - WIN_R // 2, 0), rows - ATTN_K_ROWS)
        q_rows = []
        for i in range(ATTN_Q_ROWS):
            qr = r0 + i
            rs = min(max(qr - WIN_R // 2, 0), rows - WIN_R)
            j0 = rs - start
            ro0 = rs - qr + (WIN_R - 1)
            blk = by_q[:, :, ro0 * GRID_W:(ro0 + WIN_R) * GRID_W]
            q_rows.append(jnp.pad(blk, ((0, 0), (0, 0), (j0 * GRID_W, (ATTN_K_ROWS - WIN_R - j0) * GRID_W)),
                                  constant_values=MASKED))
        tables.append(jnp.concatenate(q_rows, axis=1))
    return jnp.stack(tables)


def _attn_kernel(q_ref, k_ref, v_ref, t_ref, o_ref, *, rows):
    g = pl.program_id(2)
    start = pl.multiple_of(_attn_group_start(g, rows) * GRID_W, GRID_W)
    nk = ATTN_K_ROWS * GRID_W
    kw = k_ref[pl.ds(start, nk), :]
    vw = v_ref[pl.ds(start, nk), :]
    s = lax.dot_general(q_ref[...], kw, (((1,), (1,)), ((), ())), preferred_element_type=F32)
    s = s * (HEAD_DIM ** -0.5) + t_ref[0, 0]
    p = jnp.exp(s - jnp.max(s, axis=-1, keepdims=True))
    l = jnp.sum(p, axis=-1, keepdims=True)
    o = jnp.dot(p.astype(BF16), vw, preferred_element_type=F32)
    o_ref[...] = (o / l).astype(o_ref.dtype)


def _attention(z, tables, batch, seq):
    rows = seq // GRID_W
    n_groups = rows // ATTN_Q_ROWS
    tq = ATTN_Q_ROWS * GRID_W
    tk = ATTN_K_ROWS * GRID_W

    def table_type(g):
        return jnp.where(g == 0, 0, jnp.where(g == n_groups - 1, 2, 1))

    return pl.pallas_call(
        functools.partial(_attn_kernel, rows=rows),
        grid=(batch, N_HEADS, n_groups),
        in_specs=[pl.BlockSpec((tq, HEAD_DIM), lambda b, h, g: (b * n_groups + g, h)),
                  pl.BlockSpec((seq, HEAD_DIM), lambda b, h, g: (b, N_HEADS + h)),
                  pl.BlockSpec((seq, HEAD_DIM), lambda b, h, g: (b, 2 * N_HEADS + h)),
                  pl.BlockSpec((1, 1, tq, tk), lambda b, h, g: (table_type(g), h, 0, 0))],
        out_specs=pl.BlockSpec((tq, HEAD_DIM), lambda b, h, g: (b * n_groups + g, h)),
        out_shape=jax.ShapeDtypeStruct((batch * seq, ATTN_W), BF16),
        compiler_params=_params("parallel", "parallel", "arbitrary"),
        name="nbr_attention",
    )(z, z, z, tables)


def _pool_band_matrices():
    i = jnp.arange(POOL_CHUNK)[:, None]
    off = jnp.arange(POOL_CHUNK + 2 * POOL_HALO)[None, :] - POOL_HALO - i
    return jnp.stack([((off >= -(w // 2)) & (off <= w // 2 - 1)).astype(BF16) for w in POOL_WINDOWS])


def _pool_kernel(p_ref, band_ref, w_ref, sc_ref, o_ref, pad_ref, *, seq):
    g = pl.program_id(1)
    half = jnp.left_shift(1, g)
    cols = p_ref.shape[1]
    pad_ref[pl.ds(0, POOL_HALO), :] = jnp.zeros((POOL_HALO, cols), BF16)
    pad_ref[pl.ds(POOL_HALO + seq, POOL_HALO), :] = jnp.zeros((POOL_HALO, cols), BF16)
    pad_ref[pl.ds(POOL_HALO, seq), :] = p_ref[...]

    def chunk(c, carry):
        base = pl.multiple_of(c * POOL_CHUNK, POOL_CHUNK)
        ph = pad_ref[pl.ds(base, POOL_CHUNK + 2 * POOL_HALO), :]
        wsum = jnp.dot(band_ref[0], ph, preferred_element_type=F32)
        t = base + lax.broadcasted_iota(jnp.int32, (POOL_CHUNK, 1), 0)
        lo = jnp.maximum(t - half, 0)
        hi = jnp.minimum(t + half - 1, seq - 1)
        cnt = (hi - lo + 1).astype(F32)
        centre = ph[POOL_HALO:POOL_HALO + POOL_CHUNK].astype(F32)
        d = wsum / cnt - centre
        y = jnp.dot(d.astype(BF16), w_ref[0], preferred_element_type=F32) * sc_ref[...]
        o_ref[pl.ds(base, POOL_CHUNK), :] = y.astype(o_ref.dtype)
        return carry

    lax.fori_loop(0, seq // POOL_CHUNK, chunk, 0)


def _pool(z, bands, w_grp, scale, batch, seq, col_block0):
    n_g = len(POOL_WINDOWS)
    return pl.pallas_call(
        functools.partial(_pool_kernel, seq=seq),
        grid=(batch, n_g),
        in_specs=[pl.BlockSpec((seq, POOL_GROUP), lambda b, g: (b, col_block0 + g)),
                  pl.BlockSpec((1, POOL_CHUNK, POOL_CHUNK + 2 * POOL_HALO), lambda b, g: (g, 0, 0)),
                  pl.BlockSpec((1, POOL_GROUP, POOL_GROUP), lambda b, g: (g, 0, 0)),
                  pl.BlockSpec((1, POOL_GROUP), lambda b, g: (0, g))],
        out_specs=pl.BlockSpec((seq, POOL_GROUP), lambda b, g: (b, g)),
        out_shape=jax.ShapeDtypeStruct((batch * seq, POOL_W), BF16),
        scratch_shapes=[pltpu.VMEM((seq + 2 * POOL_HALO, POOL_GROUP), BF16)],
        compiler_params=_params("parallel", "arbitrary"),
        name="multiscale_pool",
    )(z, bands, w_grp, scale.reshape(1, POOL_W))


def _gelu_tanh(x):
    return 0.5 * x * (1.0 + jnp.tanh(0.7978845608028654 * (x + 0.044715 * (x * x * x))))


def _gmlp_kernel(u_ref, v_ref, lng_ref, lnb_ref, ws_ref, bs_ref, o_ref):
    v = _gelu_tanh(v_ref[...].astype(F32))
    mu = jnp.mean(v, axis=-1, keepdims=True)
    vc = v - mu
    var = jnp.mean(vc * vc, axis=-1, keepdims=True)
    vn = (vc * lax.rsqrt(var + EPS) * lng_ref[...] + lnb_ref[...]).astype(BF16)
    n_chunks = u_ref.shape[0] // GMLP_CHUNK
    for c in range(n_chunks):
        r = slice(c * GMLP_CHUNK, (c + 1) * GMLP_CHUNK)
        for g in range(GMLP_GROUPS):
            cs = slice(g * GMLP_GROUP_W, (g + 1) * GMLP_GROUP_W)
            mixed = jnp.dot(ws_ref[g], vn[r, cs], preferred_element_type=F32) + bs_ref[g]
            u = _gelu_tanh(u_ref[r, cs].astype(F32))
            o_ref[r, cs] = (u * mixed).astype(o_ref.dtype)


def _gmlp(z, ln_g, ln_b, w_s, b_s, u_block, tm=512):
    m = z.shape[0]
    return pl.pallas_call(
        _gmlp_kernel,
        grid=(m // tm,),
        in_specs=[pl.BlockSpec((tm, GMLP_W), lambda i: (i, u_block)),
                  pl.BlockSpec((tm, GMLP_W), lambda i: (i, u_block + 1)),
                  pl.BlockSpec((1, GMLP_W), lambda i: (0, 0)),
                  pl.BlockSpec((1, GMLP_W), lambda i: (0, 0)),
                  pl.BlockSpec((GMLP_GROUPS, GMLP_CHUNK, GMLP_CHUNK), lambda i: (0, 0, 0)),
                  pl.BlockSpec((GMLP_GROUPS, GMLP_CHUNK, 1), lambda i: (0, 0, 0))],
        out_specs=pl.BlockSpec((tm, GMLP_W), lambda i: (i, 0)),
        out_shape=jax.ShapeDtypeStruct((m, GMLP_W), BF16),
        compiler_params=_params("parallel"),
        name="spatial_gating",
    )(z, z, ln_g.reshape(1, GMLP_W), ln_b.reshape(1, GMLP_W), w_s, b_s.reshape(GMLP_GROUPS, GMLP_CHUNK, 1))


def _sigmoid(x):
    return 1.0 / (1.0 + jnp.exp(-x))


def _merge_kernel(ya_ref, yp_ref, ys_ref, hd_ref, wb_ref, gu0_ref, gu1_ref, gu2_ref,
                  gb0_ref, gb1_ref, gb2_ref, o_ref, wbb_ref, gub_ref):
    _cast_on_first_m_step([(wb_ref, wbb_ref), (gu0_ref, gub_ref.at[0]), (gu1_ref, gub_ref.at[1]),
                           (gu2_ref, gub_ref.at[2])])
    hd = hd_ref[...]
    o_pool = ATTN_W
    o_sg = ATTN_W + POOL_W
    ga = _sigmoid(jnp.dot(hd, gub_ref[0], preferred_element_type=F32) + gb0_ref[...])
    acc = ga * jnp.dot(ya_ref[...], wbb_ref[pl.ds(0, ATTN_W), :], preferred_element_type=F32)
    gp = _sigmoid(jnp.dot(hd, gub_ref[1], preferred_element_type=F32) + gb1_ref[...])
    acc = acc + gp * jnp.dot(yp_ref[...], wbb_ref[pl.ds(o_pool, POOL_W), :], preferred_element_type=F32)
    gs = _sigmoid(jnp.dot(hd, gub_ref[2], preferred_element_type=F32) + gb2_ref[...])
    acc = acc + gs * jnp.dot(ys_ref[...], wbb_ref[pl.ds(o_sg, GMLP_W), :], preferred_element_type=F32)
    o_ref[...] = acc.astype(o_ref.dtype)


def _merge(ya, yp, ys, hd, wb_stack, gate_up_stack, gate_b, layer, *, tm, tn):
    m = ya.shape[0]
    mix_w, d = wb_stack.shape[1:]
    rank = hd.shape[1]
    nb = d // tn
    row = lambda w: pl.BlockSpec((tm, w), lambda j, i: (i, 0))
    gate_b = gate_b.reshape(1, N_BRANCH * d)
    gu = lambda br: pl.BlockSpec((None, rank, tn), lambda j, i: (layer, 0, br * nb + j))
    gb = lambda br: pl.BlockSpec((1, tn), lambda j, i: (0, br * nb + j))
    return pl.pallas_call(
        _merge_kernel,
        grid=(nb, m // tm),
        in_specs=[row(ATTN_W), row(POOL_W), row(GMLP_W), row(rank),
                  pl.BlockSpec((None, mix_w, tn), lambda j, i: (layer, 0, j)),
                  gu(0), gu(1), gu(2), gb(0), gb(1), gb(2)],
        out_specs=pl.BlockSpec((tm, tn), lambda j, i: (i, j)),
        out_shape=jax.ShapeDtypeStruct((m, d), BF16),
        scratch_shapes=[pltpu.VMEM((mix_w, tn), BF16), pltpu.VMEM((N_BRANCH, rank, tn), BF16)],
        compiler_params=_params("arbitrary", "arbitrary"),
        name="branch_merge",
    )(ya, yp, ys, hd, wb_stack, gate_up_stack, gate_up_stack, gate_up_stack, gate_b, gate_b, gate_b)


def kernel(x, attn_norm_g, w_in, rpb, pool_w, pool_scale, gmlp_ln_g, gmlp_ln_b, gmlp_w_s, gmlp_b_s,
           w_branch, gate_down, gate_up, gate_b, w_out, ffn_norm_g, w_ffn_gate, w_ffn_up, w_ffn_down,
           final_norm_g):
    batch, seq, d = x.shape
    depth = w_in.shape[0]
    rows = seq // GRID_W
    xf = x.reshape(batch * seq, d)
    bands = _pool_band_matrices()
    pool_col_block = (3 * ATTN_W) // POOL_GROUP
    u_block = (3 * ATTN_W + POOL_W) // GMLP_W
    w_down_bf16 = w_ffn_down.astype(BF16)
    for l in range(depth):
        h = _rmsnorm(xf, attn_norm_g[l], BF16)
        z = _matmul_ws(h, w_in, l, tm=1024, tn=512, out_dtype=BF16, name="in_proj")
        hd = _matmul_ws(h, gate_down, l, tm=1024, tn=gate_down.shape[2], out_dtype=BF16, name="gate_down")
        y_attn = _attention(z, _attn_bias_tables(rpb[l], rows), batch, seq)
        y_pool = _pool(z, bands, pool_w[l].astype(BF16), pool_scale[l], batch, seq, pool_col_block)
        y_sg = _gmlp(z, gmlp_ln_g[l], gmlp_ln_b[l], gmlp_w_s[l].astype(BF16), gmlp_b_s[l], u_block)
        merged = _merge(y_attn, y_pool, y_sg, hd, w_branch, gate_up, gate_b[l], l, tm=1024, tn=512)
        xf = _matmul_ws(merged, w_out, l, tm=1024, tn=512, out_dtype=F32, residual=xf, name="out_proj")
        h = _rmsnorm(xf, ffn_norm_g[l], BF16)
        act = _swiglu(h, w_ffn_gate, w_ffn_up, l, tm=1024, tn=256)
        xf = _matmul_res(act, w_down_bf16[l], xf, tm=512, tn=256, name="ffn_down")
    out = _rmsnorm(xf, final_norm_g, F32)
    return out.reshape(batch, seq, d)
```

```python
import functools

import jax
import jax.numpy as jnp
from jax import lax
from jax.experimental import pallas as pl
from jax.experimental.pallas import tpu as pltpu

F32 = jnp.float32
BF16 = jnp.bfloat16

GRID_W = 64
N_HEADS = 16
HEAD_DIM = 128
ATTN_W = N_HEADS * HEAD_DIM
WIN_R = 8
WIN_C = 16
POOL_WINDOWS = (2, 4, 8, 16)
POOL_GROUP = 256
POOL_W = POOL_GROUP * len(POOL_WINDOWS)
GMLP_CHUNK = 128
GMLP_GROUPS = 4
GMLP_GROUP_W = 256
GMLP_W = GMLP_GROUPS * GMLP_GROUP_W
N_BRANCH = 3
EPS = 1e-6

VMEM_LIMIT_BYTES = 56 * 1024 * 1024

ATTN_Q_ROWS = 4
ATTN_K_ROWS = ATTN_Q_ROWS + WIN_R - 1
POOL_CHUNK = 256
POOL_HALO = 128
MASKED = -1e30


def _params(*sem):
    return pltpu.CompilerParams(dimension_semantics=sem, vmem_limit_bytes=VMEM_LIMIT_BYTES)


def _rmsnorm_kernel(x_ref, g_ref, o_ref):
    x = x_ref[...]
    ms = jnp.mean(x * x, axis=-1, keepdims=True)
    o_ref[...] = (x * lax.rsqrt(ms + EPS) * g_ref[...]).astype(o_ref.dtype)


def _rmsnorm(x, g, out_dtype, tm=256):
    m, d = x.shape
    return pl.pallas_call(
        _rmsnorm_kernel,
        grid=(m // tm,),
        in_specs=[pl.BlockSpec((tm, d), lambda i: (i, 0)),
                  pl.BlockSpec((1, d), lambda i: (0, 0))],
        out_specs=pl.BlockSpec((tm, d), lambda i: (i, 0)),
        out_shape=jax.ShapeDtypeStruct((m, d), out_dtype),
        compiler_params=_params("parallel"),
        name="rmsnorm",
    )(x, g.reshape(1, d))


def _mm_res_kernel(a_ref, w_ref, r_ref, o_ref):
    o_ref[...] = r_ref[...] + jnp.dot(a_ref[...], w_ref[...], preferred_element_type=F32)


def _matmul_res(a, w_stack, layer, residual, *, tm, tn, name):
    m, k = a.shape
    n = w_stack.shape[2]
    return pl.pallas_call(
        _mm_res_kernel,
        grid=(m // tm, n // tn),
        in_specs=[pl.BlockSpec((tm, k), lambda i, j: (i, 0)),
                  pl.BlockSpec((None, k, tn), lambda i, j: (layer, 0, j)),
                  pl.BlockSpec((tm, tn), lambda i, j: (i, j))],
        out_specs=pl.BlockSpec((tm, tn), lambda i, j: (i, j)),
        out_shape=jax.ShapeDtypeStruct((m, n), F32),
        compiler_params=_params("parallel", "arbitrary"),
        name=name,
    )(a, w_stack, residual)


def _cast_on_first_m_step(pairs):
    @pl.when(pl.program_id(1) == 0)
    def _():
        for src, dst in pairs:
            dst[...] = src[...].astype(BF16)


def _ws_mm_kernel(a_ref, w_ref, o_ref, wb_ref):
    _cast_on_first_m_step([(w_ref, wb_ref)])
    o_ref[...] = jnp.dot(a_ref[...], wb_ref[...], preferred_element_type=F32).astype(o_ref.dtype)


def _ws_mm_res_kernel(a_ref, w_ref, r_ref, o_ref, wb_ref):
    _cast_on_first_m_step([(w_ref, wb_ref)])
    o_ref[...] = r_ref[...] + jnp.dot(a_ref[...], wb_ref[...], preferred_element_type=F32)


def _matmul_ws(a, w_stack, layer, *, tm, tn, out_dtype, residual=None, name):
    m, k = a.shape
    n = w_stack.shape[2]
    in_specs = [pl.BlockSpec((tm, k), lambda j, i: (i, 0)),
                pl.BlockSpec((None, k, tn), lambda j, i: (layer, 0, j))]
    args = [a, w_stack]
    kern = _ws_mm_kernel
    if residual is not None:
        in_specs.append(pl.BlockSpec((tm, tn), lambda j, i: (i, j)))
        args.append(residual)
        kern = _ws_mm_res_kernel
    return pl.pallas_call(
        kern,
        grid=(n // tn, m // tm),
        in_specs=in_specs,
        out_specs=pl.BlockSpec((tm, tn), lambda j, i: (i, j)),
        out_shape=jax.ShapeDtypeStruct((m, n), out_dtype),
        scratch_shapes=[pltpu.VMEM((k, tn), BF16)],
        compiler_params=_params("arbitrary", "arbitrary"),
        name=name,
    )(*args)


def _swiglu_kernel(h_ref, wg_ref, wu_ref, o_ref, wgb_ref, wub_ref):
    _cast_on_first_m_step([(wg_ref, wgb_ref), (wu_ref, wub_ref)])
    h = h_ref[...]
    a = jnp.dot(h, wgb_ref[...], preferred_element_type=F32)
    b = jnp.dot(h, wub_ref[...], preferred_element_type=F32)
    o_ref[...] = (a * (1.0 / (1.0 + jnp.exp(-a))) * b).astype(o_ref.dtype)


def _swiglu(h, wg_stack, wu_stack, layer, *, tm, tn):
    m, k = h.shape
    n = wg_stack.shape[2]
    w_spec = pl.BlockSpec((None, k, tn), lambda j, i: (layer, 0, j))
    return pl.pallas_call(
        _swiglu_kernel,
        grid=(n // tn, m // tm),
        in_specs=[pl.BlockSpec((tm, k), lambda j, i: (i, 0)), w_spec, w_spec],
        out_specs=pl.BlockSpec((tm, tn), lambda j, i: (i, j)),
        out_shape=jax.ShapeDtypeStruct((m, n), BF16),
        scratch_shapes=[pltpu.VMEM((k, tn), BF16), pltpu.VMEM((k, tn), BF16)],
        compiler_params=_params("arbitrary", "arbitrary"),
        name="swiglu",
    )(h, wg_stack, wu_stack)


def _attn_group_start(g, rows):
    return min(max(g * ATTN_Q_ROWS - WIN_R // 2, 0), rows - ATTN_K_ROWS)


def _attn_group_type(g, n_groups):
    return 0 if g == 0 else (2 if g == n_groups - 1 else 1)


def _attn_bias_tables(rpb, rows):
    nl, h, n_ro, _ = rpb.shape
    n_groups = rows // ATTN_Q_ROWS
    rpb = rpb.astype(F32) * (HEAD_DIM ** 0.5)
    edge = GRID_W - WIN_C
    ext = jnp.concatenate([jnp.broadcast_to(rpb[..., :1], (nl, h, n_ro, edge)), rpb,
                           jnp.broadcast_to(rpb[..., -1:], (nl, h, n_ro, edge)),
                           jnp.zeros((nl, h, n_ro, 1), F32)], axis=-1)
    toep = jnp.tile(ext, (1, 1, 1, GRID_W))[..., :GRID_W * (2 * GRID_W - 1)]
    toep = toep.reshape(nl, h, n_ro, GRID_W, 2 * GRID_W - 1)[..., GRID_W - 1:]
    qc = jnp.arange(GRID_W)[:, None]
    kc = jnp.arange(GRID_W)[None, :]
    cs = jnp.clip(qc - WIN_C // 2, 0, GRID_W - WIN_C)
    toep = jnp.where((kc >= cs) & (kc < cs + WIN_C), toep, MASKED)
    by_q = toep.transpose(0, 1, 3, 2, 4).reshape(nl, h, GRID_W, n_ro * GRID_W)
    tables = []
    for g in (0, 1, n_groups - 1):
        r0 = g * ATTN_Q_ROWS
        start = _attn_group_start(g, rows)
        q_rows = []
        for i in range(ATTN_Q_ROWS):
            qr = r0 + i
            rs = min(max(qr - WIN_R // 2, 0), rows - WIN_R)
            j0 = rs - start
            ro0 = rs - qr + (WIN_R - 1)
            blk = by_q[..., ro0 * GRID_W:(ro0 + WIN_R) * GRID_W]
            pad = ((0, 0), (0, 0), (0, 0), (j0 * GRID_W, (ATTN_K_ROWS - WIN_R - j0) * GRID_W))
            q_rows.append(jnp.pad(blk, pad, constant_values=MASKED))
        tables.append(jnp.concatenate(q_rows, axis=2))
    return jnp.stack(tables, axis=1)


def _attn_kernel(q_ref, k_ref, v_ref, t_ref, o_ref, *, rows):
    n_groups = rows // ATTN_Q_ROWS
    tq = ATTN_Q_ROWS * GRID_W
    nk = ATTN_K_ROWS * GRID_W
    exp2_scale = (HEAD_DIM ** -0.5) * 1.4426950408889634
    for g in range(n_groups):
        start = _attn_group_start(g, rows) * GRID_W
        q = q_ref[pl.ds(g * tq, tq), :]
        kw = k_ref[pl.ds(start, nk), :]
        vw = v_ref[pl.ds(start, nk), :]
        s = lax.dot_general(q, kw, (((1,), (1,)), ((), ())), preferred_element_type=F32)
        s = s + t_ref[_attn_group_type(g, n_groups)]
        p = jnp.exp2((s - jnp.max(s, axis=-1, keepdims=True)) * exp2_scale)
        l = jnp.sum(p, axis=-1, keepdims=True)
        o = jnp.dot(p.astype(BF16), vw, preferred_element_type=F32)
        o_ref[pl.ds(g * tq, tq), :] = (o / l).astype(o_ref.dtype)


def _attention(z, tables, layer, batch, seq):
    rows = seq // GRID_W
    tq = ATTN_Q_ROWS * GRID_W
    tk = ATTN_K_ROWS * GRID_W
    return pl.pallas_call(
        functools.partial(_attn_kernel, rows=rows),
        grid=(batch, N_HEADS),
        in_specs=[pl.BlockSpec((seq, HEAD_DIM), lambda b, h: (b, h)),
                  pl.BlockSpec((seq, HEAD_DIM), lambda b, h: (b, N_HEADS + h)),
                  pl.BlockSpec((seq, HEAD_DIM), lambda b, h: (b, 2 * N_HEADS + h)),
                  pl.BlockSpec((None, 3, None, tq, tk), lambda b, h: (layer, 0, h, 0, 0))],
        out_specs=pl.BlockSpec((seq, HEAD_DIM), lambda b, h: (b, h)),
        out_shape=jax.ShapeDtypeStruct((batch * seq, ATTN_W), BF16),
        compiler_params=_params("parallel", "parallel"),
        name="nbr_attention",
    )(z, z, z, tables)


def _pool_band_matrices():
    i = jnp.arange(POOL_CHUNK)[:, None]
    off = jnp.arange(POOL_CHUNK + 2 * POOL_HALO)[None, :] - POOL_HALO - i
    return jnp.stack([((off >= -(w // 2)) & (off <= w // 2 - 1)).astype(BF16) for w in POOL_WINDOWS])


def _pool_kernel(p_ref, band_ref, w_ref, sc_ref, o_ref, pad_ref, *, seq):
    g = pl.program_id(1)
    half = jnp.left_shift(1, g)
    cols = p_ref.shape[1]
    pad_ref[pl.ds(0, POOL_HALO), :] = jnp.zeros((POOL_HALO, cols), BF16)
    pad_ref[pl.ds(POOL_HALO + seq, POOL_HALO), :] = jnp.zeros((POOL_HALO, cols), BF16)
    pad_ref[pl.ds(POOL_HALO, seq), :] = p_ref[...]

    def chunk(c, carry):
        base = pl.multiple_of(c * POOL_CHUNK, POOL_CHUNK)
        ph = pad_ref[pl.ds(base, POOL_CHUNK + 2 * POOL_HALO), :]
        wsum = jnp.dot(band_ref[0], ph, preferred_element_type=F32)
        t = base + lax.broadcasted_iota(jnp.int32, (POOL_CHUNK, 1), 0)
        lo = jnp.maximum(t - half, 0)
        hi = jnp.minimum(t + half - 1, seq - 1)
        cnt = (hi - lo + 1).astype(F32)
        centre = ph[POOL_HALO:POOL_HALO + POOL_CHUNK].astype(F32)
        d = wsum / cnt - centre
        y = jnp.dot(d.astype(BF16), w_ref[0], preferred_element_type=F32) * sc_ref[...]
        o_ref[pl.ds(base, POOL_CHUNK), :] = y.astype(o_ref.dtype)
        return carry

    lax.fori_loop(0, seq // POOL_CHUNK, chunk, 0)


def _pool(z, bands, w_grp, scale, batch, seq, col_block0):
    n_g = len(POOL_WINDOWS)
    return pl.pallas_call(
        functools.partial(_pool_kernel, seq=seq),
        grid=(batch, n_g),
        in_specs=[pl.BlockSpec((seq, POOL_GROUP), lambda b, g: (b, col_block0 + g)),
                  pl.BlockSpec((1, POOL_CHUNK, POOL_CHUNK + 2 * POOL_HALO), lambda b, g: (g, 0, 0)),
                  pl.BlockSpec((1, POOL_GROUP, POOL_GROUP), lambda b, g: (g, 0, 0)),
                  pl.BlockSpec((1, POOL_GROUP), lambda b, g: (0, g))],
        out_specs=pl.BlockSpec((seq, POOL_GROUP), lambda b, g: (b, g)),
        out_shape=jax.ShapeDtypeStruct((batch * seq, POOL_W), BF16),
        scratch_shapes=[pltpu.VMEM((seq + 2 * POOL_HALO, POOL_GROUP), BF16)],
        compiler_params=_params("parallel", "arbitrary"),
        name="multiscale_pool",
    )(z, bands, w_grp, scale.reshape(1, POOL_W))


def _gelu_tanh(x):
    return 0.5 * x * (1.0 + jnp.tanh(0.7978845608028654 * (x + 0.044715 * (x * x * x))))


def _gmlp_kernel(u_ref, v_ref, lng_ref, lnb_ref, ws_ref, bs_ref, o_ref):
    v = _gelu_tanh(v_ref[...].astype(F32))
    mu = jnp.mean(v, axis=-1, keepdims=True)
    vc = v - mu
    var = jnp.mean(vc * vc, axis=-1, keepdims=True)
    vn = (vc * lax.rsqrt(var + EPS) * lng_ref[...] + lnb_ref[...]).astype(BF16)
    n_chunks = u_ref.shape[0] // GMLP_CHUNK
    for c in range(n_chunks):
        r = slice(c * GMLP_CHUNK, (c + 1) * GMLP_CHUNK)
        for g in range(GMLP_GROUPS):
            cs = slice(g * GMLP_GROUP_W, (g + 1) * GMLP_GROUP_W)
            mixed = jnp.dot(ws_ref[g], vn[r, cs], preferred_element_type=F32) + bs_ref[g]
            u = _gelu_tanh(u_ref[r, cs].astype(F32))
            o_ref[r, cs] = (u * mixed).astype(o_ref.dtype)


def _gmlp(z, ln_g, ln_b, w_s, b_s, u_block, tm=512):
    m = z.shape[0]
    return pl.pallas_call(
        _gmlp_kernel,
        grid=(m // tm,),
        in_specs=[pl.BlockSpec((tm, GMLP_W), lambda i: (i, u_block)),
                  pl.BlockSpec((tm, GMLP_W), lambda i: (i, u_block + 1)),
                  pl.BlockSpec((1, GMLP_W), lambda i: (0, 0)),
                  pl.BlockSpec((1, GMLP_W), lambda i: (0, 0)),
                  pl.BlockSpec((GMLP_GROUPS, GMLP_CHUNK, GMLP_CHUNK), lambda i: (0, 0, 0)),
                  pl.BlockSpec((GMLP_GROUPS, GMLP_CHUNK, 1), lambda i: (0, 0, 0))],
        out_specs=pl.BlockSpec((tm, GMLP_W), lambda i: (i, 0)),
        out_shape=jax.ShapeDtypeStruct((m, GMLP_W), BF16),
        compiler_params=_params("parallel"),
        name="spatial_gating",
    )(z, z, ln_g.reshape(1, GMLP_W), ln_b.reshape(1, GMLP_W), w_s, b_s.reshape(GMLP_GROUPS, GMLP_CHUNK, 1))


def _sigmoid(x):
    return 1.0 / (1.0 + jnp.exp(-x))


def _merge_kernel(ya_ref, yp_ref, ys_ref, hd_ref, wb_ref, gu0_ref, gu1_ref, gu2_ref,
                  gb0_ref, gb1_ref, gb2_ref, o_ref, wbb_ref, gub_ref):
    _cast_on_first_m_step([(wb_ref, wbb_ref), (gu0_ref, gub_ref.at[0]), (gu1_ref, gub_ref.at[1]),
                           (gu2_ref, gub_ref.at[2])])
    hd = hd_ref[...]
    o_pool = ATTN_W
    o_sg = ATTN_W + POOL_W
    ga = _sigmoid(jnp.dot(hd, gub_ref[0], preferred_element_type=F32) + gb0_ref[...])
    acc = ga * jnp.dot(ya_ref[...], wbb_ref[pl.ds(0, ATTN_W), :], preferred_element_type=F32)
    gp = _sigmoid(jnp.dot(hd, gub_ref[1], preferred_element_type=F32) + gb1_ref[...])
    acc = acc + gp * jnp.dot(yp_ref[...], wbb_ref[pl.ds(o_pool, POOL_W), :], preferred_element_type=F32)
    gs = _sigmoid(jnp.dot(hd, gub_ref[2], preferred_element_type=F32) + gb2_ref[...])
    acc = acc + gs * jnp.dot(ys_ref[...], wbb_ref[pl.ds(o_sg, GMLP_W), :], preferred_element_type=F32)
    o_ref[...] = acc.astype(o_ref.dtype)


def _merge(ya, yp, ys, hd, wb_stack, gate_up_stack, gate_b, layer, *, tm, tn):
    m = ya.shape[0]
    mix_w, d = wb_stack.shape[1:]
    rank = hd.shape[1]
    nb = d // tn
    row = lambda w: pl.BlockSpec((tm, w), lambda j, i: (i, 0))
    gate_b = gate_b.reshape(1, N_BRANCH * d)
    gu = lambda br: pl.BlockSpec((None, rank, tn), lambda j, i: (layer, 0, br * nb + j))
    gb = lambda br: pl.BlockSpec((1, tn), lambda j, i: (0, br * nb + j))
    return pl.pallas_call(
        _merge_kernel,
        grid=(nb, m // tm),
        in_specs=[row(ATTN_W), row(POOL_W), row(GMLP_W), row(rank),
                  pl.BlockSpec((None, mix_w, tn), lambda j, i: (layer, 0, j)),
                  gu(0), gu(1), gu(2), gb(0), gb(1), gb(2)],
        out_specs=pl.BlockSpec((tm, tn), lambda j, i: (i, j)),
        out_shape=jax.ShapeDtypeStruct((m, d), BF16),
        scratch_shapes=[pltpu.VMEM((mix_w, tn), BF16), pltpu.VMEM((N_BRANCH, rank, tn), BF16)],
        compiler_params=_params("arbitrary", "arbitrary"),
        name="branch_merge",
    )(ya, yp, ys, hd, wb_stack, gate_up_stack, gate_up_stack, gate_up_stack, gate_b, gate_b, gate_b)


def kernel(x, attn_norm_g, w_in, rpb, pool_w, pool_scale, gmlp_ln_g, gmlp_ln_b, gmlp_w_s, gmlp_b_s,
           w_branch, gate_down, gate_up, gate_b, w_out, ffn_norm_g, w_ffn_gate, w_ffn_up, w_ffn_down,
           final_norm_g):
    batch, seq, d = x.shape
    depth = w_in.shape[0]
    rows = seq // GRID_W
    xf = x.reshape(batch * seq, d)
    bands = _pool_band_matrices()
    pool_col_block = (3 * ATTN_W) // POOL_GROUP
    u_block = (3 * ATTN_W + POOL_W) // GMLP_W
    w_down_bf16 = w_ffn_down.astype(BF16)
    tables = _attn_bias_tables(rpb, rows)
    for l in range(depth):
        h = _rmsnorm(xf, attn_norm_g[l], BF16)
        z = _matmul_ws(h, w_in, l, tm=1024, tn=512, out_dtype=BF16, name="in_proj")
        hd = _matmul_ws(h, gate_down, l, tm=1024, tn=gate_down.shape[2], out_dtype=BF16, name="gate_down")
        y_attn = _attention(z, tables, l, batch, seq)
        y_pool = _pool(z, bands, pool_w[l].astype(BF16), pool_scale[l], batch, seq, pool_col_block)
        y_sg = _gmlp(z, gmlp_ln_g[l], gmlp_ln_b[l], gmlp_w_s[l].astype(BF16), gmlp_b_s[l], u_block)
        merged = _merge(y_attn, y_pool, y_sg, hd, w_branch, gate_up, gate_b[l], l, tm=1024, tn=512)
        xf = _matmul_ws(merged, w_out, l, tm=1024, tn=512, out_dtype=F32, residual=xf, name="out_proj")
        h = _rmsnorm(xf, ffn_norm_g[l], BF16)
        act = _swiglu(h, w_ffn_gate, w_ffn_up, l, tm=1024, tn=256)
        xf = _matmul_res(act, w_down_bf16, l, xf, tm=512, tn=256, name="ffn_down")
    out = _rmsnorm(xf, final_norm_g, F32)
    return out.reshape(batch, seq, d)
```

```python
import functools

import jax
import jax.numpy as jnp
from jax import lax
from jax.experimental import pallas as pl
from jax.experimental.pallas import tpu as pltpu

F32 = jnp.float32
BF16 = jnp.bfloat16

GRID_W = 64
N_HEADS = 16
HEAD_DIM = 128
ATTN_W = N_HEADS * HEAD_DIM
WIN_R = 8
WIN_C = 16
POOL_WINDOWS = (2, 4, 8, 16)
POOL_GROUP = 256
POOL_W = POOL_GROUP * len(POOL_WINDOWS)
GMLP_CHUNK = 128
GMLP_GROUPS = 4
GMLP_GROUP_W = 256
GMLP_W = GMLP_GROUPS * GMLP_GROUP_W
N_BRANCH = 3
EPS = 1e-6

VMEM_LIMIT_BYTES = 56 * 1024 * 1024

ATTN_Q_ROWS = 4
ATTN_K_ROWS = ATTN_Q_ROWS + WIN_R - 1
POOL_CHUNK = 256
POOL_HALO = 128
MASKED = -1e30


def _params(*sem):
    return pltpu.CompilerParams(dimension_semantics=sem, vmem_limit_bytes=VMEM_LIMIT_BYTES)


def _rmsnorm_kernel(x_ref, g_ref, o_ref):
    x = x_ref[...]
    ms = jnp.mean(x * x, axis=-1, keepdims=True)
    o_ref[...] = (x * lax.rsqrt(ms + EPS) * g_ref[...]).astype(o_ref.dtype)


def _rmsnorm(x, g, out_dtype, tm=256):
    m, d = x.shape
    return pl.pallas_call(
        _rmsnorm_kernel,
        grid=(m // tm,),
        in_specs=[pl.BlockSpec((tm, d), lambda i: (i, 0)),
                  pl.BlockSpec((1, d), lambda i: (0, 0))],
        out_specs=pl.BlockSpec((tm, d), lambda i: (i, 0)),
        out_shape=jax.ShapeDtypeStruct((m, d), out_dtype),
        compiler_params=_params("parallel"),
        name="rmsnorm",
    )(x, g.reshape(1, d))


def _cast_on_first_m_step(pairs):
    @pl.when(pl.program_id(1) == 0)
    def _():
        for src, dst in pairs:
            dst[...] = src[...].astype(BF16)


def _ws_mm_kernel(a_ref, w_ref, o_ref, wb_ref):
    _cast_on_first_m_step([(w_ref, wb_ref)])
    o_ref[...] = jnp.dot(a_ref[...], wb_ref[...], preferred_element_type=F32).astype(o_ref.dtype)


def _ws_mm_res_kernel(a_ref, w_ref, r_ref, o_ref, wb_ref):
    _cast_on_first_m_step([(w_ref, wb_ref)])
    o_ref[...] = r_ref[...] + jnp.dot(a_ref[...], wb_ref[...], preferred_element_type=F32)


def _matmul_ws(a, w_stack, layer, *, tm, tn, out_dtype, residual=None, name):
    m, k = a.shape
    n = w_stack.shape[2]
    in_specs = [pl.BlockSpec((tm, k), lambda j, i: (i, 0)),
                pl.BlockSpec((None, k, tn), lambda j, i: (layer, 0, j))]
    args = [a, w_stack]
    kern = _ws_mm_kernel
    if residual is not None:
        in_specs.append(pl.BlockSpec((tm, tn), lambda j, i: (i, j)))
        args.append(residual)
        kern = _ws_mm_res_kernel
    return pl.pallas_call(
        kern,
        grid=(n // tn, m // tm),
        in_specs=in_specs,
        out_specs=pl.BlockSpec((tm, tn), lambda j, i: (i, j)),
        out_shape=jax.ShapeDtypeStruct((m, n), out_dtype),
        scratch_shapes=[pltpu.VMEM((k, tn), BF16)],
        compiler_params=_params("arbitrary", "arbitrary"),
        name=name,
    )(*args)


def _pw_fill(w_refs, slot_refs, rows):
    r0 = pl.multiple_of(pl.program_id(1) * rows, rows)
    for w_ref, slot_ref in zip(w_refs, slot_refs):
        slot_ref[pl.ds(r0, rows), :] = w_ref[...].astype(BF16)


def _pw_phases(w_refs, even_slots, odd_slots, rows, compute):
    j = pl.program_id(0)

    @pl.when(j == 0)
    def _():
        _pw_fill(w_refs, even_slots, rows)

    @pl.when((j > 0) & (j % 2 == 1))
    def _():
        _pw_fill(w_refs, odd_slots, rows)
        compute(even_slots)

    @pl.when((j > 0) & (j % 2 == 0))
    def _():
        _pw_fill(w_refs, even_slots, rows)
        compute(odd_slots)


def _pw_mm_kernel(a_ref, w_ref, o_ref, w0_ref, w1_ref, *, rows):
    def compute(slots):
        o_ref[...] = jnp.dot(a_ref[...], slots[0][...], preferred_element_type=F32).astype(o_ref.dtype)
    _pw_phases([w_ref], [w0_ref], [w1_ref], rows, compute)


def _pw_mm_res_kernel(a_ref, w_ref, r_ref, o_ref, w0_ref, w1_ref, *, rows):
    def compute(slots):
        o_ref[...] = r_ref[...] + jnp.dot(a_ref[...], slots[0][...], preferred_element_type=F32)
    _pw_phases([w_ref], [w0_ref], [w1_ref], rows, compute)


def _pw_swiglu_kernel(h_ref, wg_ref, wu_ref, o_ref, g0_ref, u0_ref, g1_ref, u1_ref, *, rows):
    def compute(slots):
        h = h_ref[...]
        a = jnp.dot(h, slots[0][...], preferred_element_type=F32)
        b = jnp.dot(h, slots[1][...], preferred_element_type=F32)
        o_ref[...] = (a * (1.0 / (1.0 + jnp.exp(-a))) * b).astype(o_ref.dtype)
    _pw_phases([wg_ref, wu_ref], [g0_ref, u0_ref], [g1_ref, u1_ref], rows, compute)


def _pw_call(kern, a, w_stacks, layer, *, tm, tn, out_dtype, residual=None, name):
    m, k = a.shape
    n = w_stacks[0].shape[2]
    n_n, n_m = n // tn, m // tm
    rows = k // n_m
    assert k % n_m == 0 and rows % 16 == 0, (k, n_m)
    m_idx = lambda j, i: jnp.where(j == 0, 0, i)
    w_spec = pl.BlockSpec((None, rows, tn),
                          lambda j, i: (layer, jnp.where(j < n_n, i, n_m - 1), jnp.minimum(j, n_n - 1)))
    o_spec = pl.BlockSpec((tm, tn), lambda j, i: (m_idx(j, i), jnp.maximum(j - 1, 0)))
    in_specs = [pl.BlockSpec((tm, k), lambda j, i: (m_idx(j, i), 0))] + [w_spec] * len(w_stacks)
    args = [a, *w_stacks]
    if residual is not None:
        in_specs.append(o_spec)
        args.append(residual)
    return pl.pallas_call(
        functools.partial(kern, rows=rows),
        grid=(n_n + 1, n_m),
        in_specs=in_specs,
        out_specs=o_spec,
        out_shape=jax.ShapeDtypeStruct((m, n), out_dtype),
        scratch_shapes=[pltpu.VMEM((k, tn), BF16)] * (2 * len(w_stacks)),
        compiler_params=_params("arbitrary", "arbitrary"),
        name=name,
    )(*args)


def _attn_group_start(g, rows):
    return min(max(g * ATTN_Q_ROWS - WIN_R // 2, 0), rows - ATTN_K_ROWS)


def _attn_group_type(g, n_groups):
    return 0 if g == 0 else (2 if g == n_groups - 1 else 1)


def _attn_bias_tables(rpb, rows):
    nl, h, n_ro, _ = rpb.shape
    n_groups = rows // ATTN_Q_ROWS
    rpb = rpb.astype(F32) * (HEAD_DIM ** 0.5)
    edge = GRID_W - WIN_C
    ext = jnp.concatenate([jnp.broadcast_to(rpb[..., :1], (nl, h, n_ro, edge)), rpb,
                           jnp.broadcast_to(rpb[..., -1:], (nl, h, n_ro, edge)),
                           jnp.zeros((nl, h, n_ro, 1), F32)], axis=-1)
    toep = jnp.tile(ext, (1, 1, 1, GRID_W))[..., :GRID_W * (2 * GRID_W - 1)]
    toep = toep.reshape(nl, h, n_ro, GRID_W, 2 * GRID_W - 1)[..., GRID_W - 1:]
    qc = jnp.arange(GRID_W)[:, None]
    kc = jnp.arange(GRID_W)[None, :]
    cs = jnp.clip(qc - WIN_C // 2, 0, GRID_W - WIN_C)
    toep = jnp.where((kc >= cs) & (kc < cs + WIN_C), toep, MASKED)
    by_q = toep.transpose(0, 1, 3, 2, 4).reshape(nl, h, GRID_W, n_ro * GRID_W)
    tables = []
    for g in (0, 1, n_groups - 1):
        r0 = g * ATTN_Q_ROWS
        start = _attn_group_start(g, rows)
        q_rows = []
        for i in range(ATTN_Q_ROWS):
            qr = r0 + i
            rs = min(max(qr - WIN_R // 2, 0), rows - WIN_R)
            j0 = rs - start
            ro0 = rs - qr + (WIN_R - 1)
            blk = by_q[..., ro0 * GRID_W:(ro0 + WIN_R) * GRID_W]
            pad = ((0, 0), (0, 0), (0, 0), (j0 * GRID_W, (ATTN_K_ROWS - WIN_R - j0) * GRID_W))
            q_rows.append(jnp.pad(blk, pad, constant_values=MASKED))
        tables.append(jnp.concatenate(q_rows, axis=2))
    return jnp.stack(tables, axis=1)


def _attn_kernel(q_ref, k_ref, v_ref, t_ref, o_ref, *, rows):
    n_groups = rows // ATTN_Q_ROWS
    tq = ATTN_Q_ROWS * GRID_W
    nk = ATTN_K_ROWS * GRID_W
    exp2_scale = (HEAD_DIM ** -0.5) * 1.4426950408889634
    for g in range(n_groups):
        start = _attn_group_start(g, rows) * GRID_W
        q = q_ref[pl.ds(g * tq, tq), :]
        kw = k_ref[pl.ds(start, nk), :]
        vw = v_ref[pl.ds(start, nk), :]
        s = lax.dot_general(q, kw, (((1,), (1,)), ((), ())), preferred_element_type=F32)
        s = s + t_ref[_attn_group_type(g, n_groups)]
        p = jnp.exp2((s - jnp.max(s, axis=-1, keepdims=True)) * exp2_scale)
        l = jnp.sum(p, axis=-1, keepdims=True)
        o = jnp.dot(p.astype(BF16), vw, preferred_element_type=F32)
        o_ref[pl.ds(g * tq, tq), :] = (o / l).astype(o_ref.dtype)


def _attention(z, tables, layer, batch, seq):
    rows = seq // GRID_W
    tq = ATTN_Q_ROWS * GRID_W
    tk = ATTN_K_ROWS * GRID_W
    return pl.pallas_call(
        functools.partial(_attn_kernel, rows=rows),
        grid=(batch, N_HEADS),
        in_specs=[pl.BlockSpec((seq, HEAD_DIM), lambda b, h: (b, h)),
                  pl.BlockSpec((seq, HEAD_DIM), lambda b, h: (b, N_HEADS + h)),
                  pl.BlockSpec((seq, HEAD_DIM), lambda b, h: (b, 2 * N_HEADS + h)),
                  pl.BlockSpec((None, 3, None, tq, tk), lambda b, h: (layer, 0, h, 0, 0))],
        out_specs=pl.BlockSpec((seq, HEAD_DIM), lambda b, h: (b, h)),
        out_shape=jax.ShapeDtypeStruct((batch * seq, ATTN_W), BF16),
        compiler_params=_params("parallel", "parallel"),
        name="nbr_attention",
    )(z, z, z, tables)


def _pool_band_matrices():
    i = jnp.arange(POOL_CHUNK)[:, None]
    off = jnp.arange(POOL_CHUNK + 2 * POOL_HALO)[None, :] - POOL_HALO - i
    return jnp.stack([((off >= -(w // 2)) & (off <= w // 2 - 1)).astype(BF16) for w in POOL_WINDOWS])


def _pool_kernel(p_ref, band_ref, w_ref, sc_ref, o_ref, pad_ref, *, seq):
    g = pl.program_id(1)
    half = jnp.left_shift(1, g)
    cols = p_ref.shape[1]
    pad_ref[pl.ds(0, POOL_HALO), :] = jnp.zeros((POOL_HALO, cols), BF16)
    pad_ref[pl.ds(POOL_HALO + seq, POOL_HALO), :] = jnp.zeros((POOL_HALO, cols), BF16)
    pad_ref[pl.ds(POOL_HALO, seq), :] = p_ref[...]

    def chunk(c, carry):
        base = pl.multiple_of(c * POOL_CHUNK, POOL_CHUNK)
        ph = pad_ref[pl.ds(base, POOL_CHUNK + 2 * POOL_HALO), :]
        wsum = jnp.dot(band_ref[0], ph, preferred_element_type=F32)
        t = base + lax.broadcasted_iota(jnp.int32, (POOL_CHUNK, 1), 0)
        lo = jnp.maximum(t - half, 0)
        hi = jnp.minimum(t + half - 1, seq - 1)
        cnt = (hi - lo + 1).astype(F32)
        centre = ph[POOL_HALO:POOL_HALO + POOL_CHUNK].astype(F32)
        d = wsum / cnt - centre
        y = jnp.dot(d.astype(BF16), w_ref[0], preferred_element_type=F32) * sc_ref[...]
        o_ref[pl.ds(base, POOL_CHUNK), :] = y.astype(o_ref.dtype)
        return carry

    lax.fori_loop(0, seq // POOL_CHUNK, chunk, 0)


def _pool(z, bands, w_grp, scale, batch, seq, col_block0):
    n_g = len(POOL_WINDOWS)
    return pl.pallas_call(
        functools.partial(_pool_kernel, seq=seq),
        grid=(batch, n_g),
        in_specs=[pl.BlockSpec((seq, POOL_GROUP), lambda b, g: (b, col_block0 + g)),
                  pl.BlockSpec((1, POOL_CHUNK, POOL_CHUNK + 2 * POOL_HALO), lambda b, g: (g, 0, 0)),
                  pl.BlockSpec((1, POOL_GROUP, POOL_GROUP), lambda b, g: (g, 0, 0)),
                  pl.BlockSpec((1, POOL_GROUP), lambda b, g: (0, g))],
        out_specs=pl.BlockSpec((seq, POOL_GROUP), lambda b, g: (b, g)),
        out_shape=jax.ShapeDtypeStruct((batch * seq, POOL_W), BF16),
        scratch_shapes=[pltpu.VMEM((seq + 2 * POOL_HALO, POOL_GROUP), BF16)],
        compiler_params=_params("parallel", "arbitrary"),
        name="multiscale_pool",
    )(z, bands, w_grp, scale.reshape(1, POOL_W))


def _gelu_tanh(x):
    return 0.5 * x * (1.0 + jnp.tanh(0.7978845608028654 * (x + 0.044715 * (x * x * x))))


def _gmlp_kernel(u_ref, v_ref, lng_ref, lnb_ref, ws_ref, bs_ref, o_ref):
    v = _gelu_tanh(v_ref[...].astype(F32))
    mu = jnp.mean(v, axis=-1, keepdims=True)
    vc = v - mu
    var = jnp.mean(vc * vc, axis=-1, keepdims=True)
    vn = (vc * lax.rsqrt(var + EPS) * lng_ref[...] + lnb_ref[...]).astype(BF16)
    n_chunks = u_ref.shape[0] // GMLP_CHUNK
    for c in range(n_chunks):
        r = slice(c * GMLP_CHUNK, (c + 1) * GMLP_CHUNK)
        for g in range(GMLP_GROUPS):
            cs = slice(g * GMLP_GROUP_W, (g + 1) * GMLP_GROUP_W)
            mixed = jnp.dot(ws_ref[g], vn[r, cs], preferred_element_type=F32) + bs_ref[g]
            u = _gelu_tanh(u_ref[r, cs].astype(F32))
            o_ref[r, cs] = (u * mixed).astype(o_ref.dtype)


def _gmlp(z, ln_g, ln_b, w_s, b_s, u_block, tm=512):
    m = z.shape[0]
    return pl.pallas_call(
        _gmlp_kernel,
        grid=(m // tm,),
        in_specs=[pl.BlockSpec((tm, GMLP_W), lambda i: (i, u_block)),
                  pl.BlockSpec((tm, GMLP_W), lambda i: (i, u_block + 1)),
                  pl.BlockSpec((1, GMLP_W), lambda i: (0, 0)),
                  pl.BlockSpec((1, GMLP_W), lambda i: (0, 0)),
                  pl.BlockSpec((GMLP_GROUPS, GMLP_CHUNK, GMLP_CHUNK), lambda i: (0, 0, 0)),
                  pl.BlockSpec((GMLP_GROUPS, GMLP_CHUNK, 1), lambda i: (0, 0, 0))],
        out_specs=pl.BlockSpec((tm, GMLP_W), lambda i: (i, 0)),
        out_shape=jax.ShapeDtypeStruct((m, GMLP_W), BF16),
        compiler_params=_params("parallel"),
        name="spatial_gating",
    )(z, z, ln_g.reshape(1, GMLP_W), ln_b.reshape(1, GMLP_W), w_s, b_s.reshape(GMLP_GROUPS, GMLP_CHUNK, 1))


def _sigmoid(x):
    return 1.0 / (1.0 + jnp.exp(-x))


def _merge_kernel(ya_ref, yp_ref, ys_ref, hd_ref, wb_ref, gu0_ref, gu1_ref, gu2_ref,
                  gb0_ref, gb1_ref, gb2_ref, o_ref, wbb_ref, gub_ref):
    _cast_on_first_m_step([(wb_ref, wbb_ref), (gu0_ref, gub_ref.at[0]), (gu1_ref, gub_ref.at[1]),
                           (gu2_ref, gub_ref.at[2])])
    hd = hd_ref[...]
    o_pool = ATTN_W
    o_sg = ATTN_W + POOL_W
    ga = _sigmoid(jnp.dot(hd, gub_ref[0], preferred_element_type=F32) + gb0_ref[...])
    acc = ga * jnp.dot(ya_ref[...], wbb_ref[pl.ds(0, ATTN_W), :], preferred_element_type=F32)
    gp = _sigmoid(jnp.dot(hd, gub_ref[1], preferred_element_type=F32) + gb1_ref[...])
    acc = acc + gp * jnp.dot(yp_ref[...], wbb_ref[pl.ds(o_pool, POOL_W), :], preferred_element_type=F32)
    gs = _sigmoid(jnp.dot(hd, gub_ref[2], preferred_element_type=F32) + gb2_ref[...])
    acc = acc + gs * jnp.dot(ys_ref[...], wbb_ref[pl.ds(o_sg, GMLP_W), :], preferred_element_type=F32)
    o_ref[...] = acc.astype(o_ref.dtype)


def _merge(ya, yp, ys, hd, wb_stack, gate_up_stack, gate_b, layer, *, tm, tn):
    m = ya.shape[0]
    mix_w, d = wb_stack.shape[1:]
    rank = hd.shape[1]
    nb = d // tn
    row = lambda w: pl.BlockSpec((tm, w), lambda j, i: (i, 0))
    gate_b = gate_b.reshape(1, N_BRANCH * d)
    gu = lambda br: pl.BlockSpec((None, rank, tn), lambda j, i: (layer, 0, br * nb + j))
    gb = lambda br: pl.BlockSpec((1, tn), lambda j, i: (0, br * nb + j))
    return pl.pallas_call(
        _merge_kernel,
        grid=(nb, m // tm),
        in_specs=[row(ATTN_W), row(POOL_W), row(GMLP_W), row(rank),
                  pl.BlockSpec((None, mix_w, tn), lambda j, i: (layer, 0, j)),
                  gu(0), gu(1), gu(2), gb(0), gb(1), gb(2)],
        out_specs=pl.BlockSpec((tm, tn), lambda j, i: (i, j)),
        out_shape=jax.ShapeDtypeStruct((m, d), BF16),
        scratch_shapes=[pltpu.VMEM((mix_w, tn), BF16), pltpu.VMEM((N_BRANCH, rank, tn), BF16)],
        compiler_params=_params("arbitrary", "arbitrary"),
        name="branch_merge",
    )(ya, yp, ys, hd, wb_stack, gate_up_stack, gate_up_stack, gate_up_stack, gate_b, gate_b, gate_b)


def kernel(x, attn_norm_g, w_in, rpb, pool_w, pool_scale, gmlp_ln_g, gmlp_ln_b, gmlp_w_s, gmlp_b_s,
           w_branch, gate_down, gate_up, gate_b, w_out, ffn_norm_g, w_ffn_gate, w_ffn_up, w_ffn_down,
           final_norm_g):
    batch, seq, d = x.shape
    depth = w_in.shape[0]
    rows = seq // GRID_W
    xf = x.reshape(batch * seq, d)
    bands = _pool_band_matrices()
    pool_col_block = (3 * ATTN_W) // POOL_GROUP
    u_block = (3 * ATTN_W + POOL_W) // GMLP_W
    tables = _attn_bias_tables(rpb, rows)
    for l in range(depth):
        h = _rmsnorm(xf, attn_norm_g[l], BF16)
        z = _pw_call(_pw_mm_kernel, h, [w_in], l, tm=1024, tn=1024, out_dtype=BF16, name="in_proj")
        hd = _matmul_ws(h, gate_down, l, tm=1024, tn=gate_down.shape[2], out_dtype=BF16, name="gate_down")
        y_attn = _attention(z, tables, l, batch, seq)
        y_pool = _pool(z, bands, pool_w[l].astype(BF16), pool_scale[l], batch, seq, pool_col_block)
        y_sg = _gmlp(z, gmlp_ln_g[l], gmlp_ln_b[l], gmlp_w_s[l].astype(BF16), gmlp_b_s[l], u_block)
        merged = _merge(y_attn, y_pool, y_sg, hd, w_branch, gate_up, gate_b[l], l, tm=1024, tn=512)
        xf = _pw_call(_pw_mm_res_kernel, merged, [w_out], l, tm=1024, tn=512, out_dtype=F32, residual=xf,
                      name="out_proj")
        h = _rmsnorm(xf, ffn_norm_g[l], BF16)
        act = _pw_call(_pw_swiglu_kernel, h, [w_ffn_gate, w_ffn_up], l, tm=2048, tn=256, out_dtype=BF16,
                       name="swiglu")
        xf = _pw_call(_pw_mm_res_kernel, act, [w_ffn_down], l, tm=512, tn=512, out_dtype=F32, residual=xf,
                      name="ffn_down")
    out = _rmsnorm(xf, final_norm_g, F32)
    return out.reshape(batch, seq, d)
```

```python
import functools

import jax
import jax.numpy as jnp
from jax import lax
from jax.experimental import pallas as pl
from jax.experimental.pallas import tpu as pltpu

F32 = jnp.float32
BF16 = jnp.bfloat16

GRID_W = 64
N_HEADS = 16
HEAD_DIM = 128
ATTN_W = N_HEADS * HEAD_DIM
WIN_R = 8
WIN_C = 16
POOL_WINDOWS = (2, 4, 8, 16)
POOL_GROUP = 256
POOL_W = POOL_GROUP * len(POOL_WINDOWS)
GMLP_CHUNK = 128
GMLP_GROUPS = 4
GMLP_GROUP_W = 256
GMLP_W = GMLP_GROUPS * GMLP_GROUP_W
N_BRANCH = 3
EPS = 1e-6

VMEM_LIMIT_BYTES = 56 * 1024 * 1024

ATTN_Q_ROWS = 4
ATTN_K_ROWS = ATTN_Q_ROWS + WIN_R - 1
POOL_CHUNK = 256
POOL_HALO = 128
MASKED = -1e30


def _params(*sem):
    return pltpu.CompilerParams(dimension_semantics=sem, vmem_limit_bytes=VMEM_LIMIT_BYTES)


def _rmsnorm_kernel(x_ref, g_ref, o_ref):
    x = x_ref[...]
    ms = jnp.mean(x * x, axis=-1, keepdims=True)
    o_ref[...] = (x * lax.rsqrt(ms + EPS) * g_ref[...]).astype(o_ref.dtype)


def _rmsnorm(x, g, out_dtype, tm=256):
    m, d = x.shape
    return pl.pallas_call(
        _rmsnorm_kernel,
        grid=(m // tm,),
        in_specs=[pl.BlockSpec((tm, d), lambda i: (i, 0)),
                  pl.BlockSpec((1, d), lambda i: (0, 0))],
        out_specs=pl.BlockSpec((tm, d), lambda i: (i, 0)),
        out_shape=jax.ShapeDtypeStruct((m, d), out_dtype),
        compiler_params=_params("parallel"),
        name="rmsnorm",
    )(x, g.reshape(1, d))


def _cast_on_first_m_step(pairs):
    @pl.when(pl.program_id(1) == 0)
    def _():
        for src, dst in pairs:
            dst[...] = src[...].astype(BF16)


def _ws_mm_kernel(a_ref, w_ref, o_ref, wb_ref):
    _cast_on_first_m_step([(w_ref, wb_ref)])
    o_ref[...] = jnp.dot(a_ref[...], wb_ref[...], preferred_element_type=F32).astype(o_ref.dtype)


def _ws_mm_res_kernel(a_ref, w_ref, r_ref, o_ref, wb_ref):
    _cast_on_first_m_step([(w_ref, wb_ref)])
    o_ref[...] = r_ref[...] + jnp.dot(a_ref[...], wb_ref[...], preferred_element_type=F32)


def _matmul_ws(a, w_stack, layer, *, tm, tn, out_dtype, residual=None, name):
    m, k = a.shape
    n = w_stack.shape[2]
    in_specs = [pl.BlockSpec((tm, k), lambda j, i: (i, 0)),
                pl.BlockSpec((None, k, tn), lambda j, i: (layer, 0, j))]
    args = [a, w_stack]
    kern = _ws_mm_kernel
    if residual is not None:
        in_specs.append(pl.BlockSpec((tm, tn), lambda j, i: (i, j)))
        args.append(residual)
        kern = _ws_mm_res_kernel
    return pl.pallas_call(
        kern,
        grid=(n // tn, m // tm),
        in_specs=in_specs,
        out_specs=pl.BlockSpec((tm, tn), lambda j, i: (i, j)),
        out_shape=jax.ShapeDtypeStruct((m, n), out_dtype),
        scratch_shapes=[pltpu.VMEM((k, tn), BF16)],
        compiler_params=_params("arbitrary", "arbitrary"),
        name=name,
    )(*args)


def _pw_fill(w_refs, slot_refs, rows):
    r0 = pl.multiple_of(pl.program_id(1) * rows, rows)
    for w_ref, slot_ref in zip(w_refs, slot_refs):
        slot_ref[pl.ds(r0, rows), :] = w_ref[...].astype(BF16)


def _pw_phases(w_refs, even_slots, odd_slots, rows, compute):
    j = pl.program_id(0)

    @pl.when(j == 0)
    def _():
        _pw_fill(w_refs, even_slots, rows)

    @pl.when((j > 0) & (j % 2 == 1))
    def _():
        _pw_fill(w_refs, odd_slots, rows)
        compute(even_slots)

    @pl.when((j > 0) & (j % 2 == 0))
    def _():
        _pw_fill(w_refs, even_slots, rows)
        compute(odd_slots)


def _pw_mm_kernel(a_ref, w_ref, o_ref, w0_ref, w1_ref, *, rows):
    def compute(slots):
        o_ref[...] = jnp.dot(a_ref[...], slots[0][...], preferred_element_type=F32).astype(o_ref.dtype)
    _pw_phases([w_ref], [w0_ref], [w1_ref], rows, compute)


def _pw_mm_res_kernel(a_ref, w_ref, r_ref, o_ref, w0_ref, w1_ref, *, rows):
    def compute(slots):
        o_ref[...] = r_ref[...] + jnp.dot(a_ref[...], slots[0][...], preferred_element_type=F32)
    _pw_phases([w_ref], [w0_ref], [w1_ref], rows, compute)


def _pw_swiglu_kernel(h_ref, wg_ref, wu_ref, o_ref, g0_ref, u0_ref, g1_ref, u1_ref, *, rows):
    def compute(slots):
        h = h_ref[...]
        a = jnp.dot(h, slots[0][...], preferred_element_type=F32)
        b = jnp.dot(h, slots[1][...], preferred_element_type=F32)
        o_ref[...] = (a * (1.0 / (1.0 + jnp.exp(-a))) * b).astype(o_ref.dtype)
    _pw_phases([wg_ref, wu_ref], [g0_ref, u0_ref], [g1_ref, u1_ref], rows, compute)


def _pw_call(kern, a, w_stacks, layer, *, tm, tn, out_dtype, residual=None, name):
    m, k = a.shape
    n = w_stacks[0].shape[2]
    n_n, n_m = n // tn, m // tm
    rows = k // n_m
    assert k % n_m == 0 and rows % 16 == 0, (k, n_m)
    m_idx = lambda j, i: jnp.where(j == 0, 0, i)
    w_spec = pl.BlockSpec((None, rows, tn),
                          lambda j, i: (layer, jnp.where(j < n_n, i, n_m - 1), jnp.minimum(j, n_n - 1)))
    o_spec = pl.BlockSpec((tm, tn), lambda j, i: (m_idx(j, i), jnp.maximum(j - 1, 0)))
    in_specs = [pl.BlockSpec((tm, k), lambda j, i: (m_idx(j, i), 0))] + [w_spec] * len(w_stacks)
    args = [a, *w_stacks]
    if residual is not None:
        in_specs.append(o_spec)
        args.append(residual)
    return pl.pallas_call(
        functools.partial(kern, rows=rows),
        grid=(n_n + 1, n_m),
        in_specs=in_specs,
        out_specs=o_spec,
        out_shape=jax.ShapeDtypeStruct((m, n), out_dtype),
        scratch_shapes=[pltpu.VMEM((k, tn), BF16)] * (2 * len(w_stacks)),
        compiler_params=_params("arbitrary", "arbitrary"),
        name=name,
    )(*args)


def _attn_group_start(g, rows):
    return min(max(g * ATTN_Q_ROWS - WIN_R // 2, 0), rows - ATTN_K_ROWS)


def _attn_group_type(g, n_groups):
    return 0 if g == 0 else (2 if g == n_groups - 1 else 1)


ATTN_SRC_LEFT = ATTN_Q_ROWS - 1
ATTN_SRC_W = -(-((WIN_R - 1 + ATTN_SRC_LEFT + ATTN_K_ROWS) * GRID_W) // 128) * 128


def _attn_bias_source(rpb):
    nl, h, n_ro, _ = rpb.shape
    rpb = rpb.astype(F32) * (HEAD_DIM ** 0.5)
    edge = GRID_W - WIN_C
    ext = jnp.concatenate([jnp.broadcast_to(rpb[..., :1], (nl, h, n_ro, edge)), rpb,
                           jnp.broadcast_to(rpb[..., -1:], (nl, h, n_ro, edge)),
                           jnp.zeros((nl, h, n_ro, 1), F32)], axis=-1)
    toep = jnp.tile(ext, (1, 1, 1, GRID_W))[..., :GRID_W * (2 * GRID_W - 1)]
    toep = toep.reshape(nl, h, n_ro, GRID_W, 2 * GRID_W - 1)[..., GRID_W - 1:]
    qc = jnp.arange(GRID_W)[:, None]
    kc = jnp.arange(GRID_W)[None, :]
    cs = jnp.clip(qc - WIN_C // 2, 0, GRID_W - WIN_C)
    toep = jnp.where((kc >= cs) & (kc < cs + WIN_C), toep, MASKED)
    by_q = toep.transpose(0, 1, 3, 2, 4).reshape(nl, h, GRID_W, n_ro * GRID_W)
    copies = []
    for shift in (0, 1):
        left = (ATTN_SRC_LEFT - shift) * GRID_W
        pad = ((0, 0), (0, 0), (0, 0), (left, ATTN_SRC_W - left - n_ro * GRID_W))
        copies.append(jnp.pad(by_q, pad, constant_values=MASKED))
    return jnp.stack(copies, axis=2)


def _attn_fill_tables(src_ref, tab_ref, rows):
    n_groups = rows // ATTN_Q_ROWS
    nk = ATTN_K_ROWS * GRID_W
    lane = lax.broadcasted_iota(jnp.int32, (GRID_W, nk), 1)
    for t, g in enumerate((0, 1, n_groups - 1)):
        start = _attn_group_start(g, rows)
        for i in range(ATTN_Q_ROWS):
            qr = g * ATTN_Q_ROWS + i
            rs = min(max(qr - WIN_R // 2, 0), rows - WIN_R)
            lo = (rs - start) * GRID_W
            u = start - qr + (WIN_R - 1) + ATTN_SRC_LEFT
            copy = u % 2
            window = src_ref[copy, :, pl.ds((u - copy) * GRID_W, nk)]
            valid = (lane >= lo) & (lane < lo + WIN_R * GRID_W)
            tab_ref[t, pl.ds(i * GRID_W, GRID_W), :] = jnp.where(valid, window, MASKED)


def _attn_kernel(q_ref, k_ref, v_ref, src_ref, o_ref, tab_ref, *, rows):
    n_groups = rows // ATTN_Q_ROWS
    tq = ATTN_Q_ROWS * GRID_W
    nk = ATTN_K_ROWS * GRID_W
    exp2_scale = (HEAD_DIM ** -0.5) * 1.4426950408889634
    _attn_fill_tables(src_ref, tab_ref, rows)
    for g in range(n_groups):
        start = _attn_group_start(g, rows) * GRID_W
        q = q_ref[pl.ds(g * tq, tq), :]
        kw = k_ref[pl.ds(start, nk), :]
        vw = v_ref[pl.ds(start, nk), :]
        s = lax.dot_general(q, kw, (((1,), (1,)), ((), ())), preferred_element_type=F32)
        s = s + tab_ref[_attn_group_type(g, n_groups)]
        p = jnp.exp2((s - jnp.max(s, axis=-1, keepdims=True)) * exp2_scale)
        l = jnp.sum(p, axis=-1, keepdims=True)
        o = jnp.dot(p.astype(BF16), vw, preferred_element_type=F32)
        o_ref[pl.ds(g * tq, tq), :] = (o / l).astype(o_ref.dtype)


def _attention(z, bias_src, layer, batch, seq):
    rows = seq // GRID_W
    tq = ATTN_Q_ROWS * GRID_W
    tk = ATTN_K_ROWS * GRID_W
    return pl.pallas_call(
        functools.partial(_attn_kernel, rows=rows),
        grid=(batch, N_HEADS),
        in_specs=[pl.BlockSpec((seq, HEAD_DIM), lambda b, h: (b, h)),
                  pl.BlockSpec((seq, HEAD_DIM), lambda b, h: (b, N_HEADS + h)),
                  pl.BlockSpec((seq, HEAD_DIM), lambda b, h: (b, 2 * N_HEADS + h)),
                  pl.BlockSpec((None, None, 2, GRID_W, ATTN_SRC_W), lambda b, h: (layer, h, 0, 0, 0))],
        out_specs=pl.BlockSpec((seq, HEAD_DIM), lambda b, h: (b, h)),
        out_shape=jax.ShapeDtypeStruct((batch * seq, ATTN_W), BF16),
        scratch_shapes=[pltpu.VMEM((3, tq, tk), F32)],
        compiler_params=_params("parallel", "parallel"),
        name="nbr_attention",
    )(z, z, z, bias_src)


def _pool_band_matrices():
    i = jnp.arange(POOL_CHUNK)[:, None]
    off = jnp.arange(POOL_CHUNK + 2 * POOL_HALO)[None, :] - POOL_HALO - i
    return jnp.stack([((off >= -(w // 2)) & (off <= w // 2 - 1)).astype(BF16) for w in POOL_WINDOWS])


def _pool_kernel(p_ref, band_ref, w_ref, sc_ref, o_ref, pad_ref, *, seq):
    g = pl.program_id(1)
    half = jnp.left_shift(1, g)
    cols = p_ref.shape[1]
    pad_ref[pl.ds(0, POOL_HALO), :] = jnp.zeros((POOL_HALO, cols), BF16)
    pad_ref[pl.ds(POOL_HALO + seq, POOL_HALO), :] = jnp.zeros((POOL_HALO, cols), BF16)
    pad_ref[pl.ds(POOL_HALO, seq), :] = p_ref[...]

    def chunk(c, carry):
        base = pl.multiple_of(c * POOL_CHUNK, POOL_CHUNK)
        ph = pad_ref[pl.ds(base, POOL_CHUNK + 2 * POOL_HALO), :]
        wsum = jnp.dot(band_ref[0], ph, preferred_element_type=F32)
        t = base + lax.broadcasted_iota(jnp.int32, (POOL_CHUNK, 1), 0)
        lo = jnp.maximum(t - half, 0)
        hi = jnp.minimum(t + half - 1, seq - 1)
        cnt = (hi - lo + 1).astype(F32)
        centre = ph[POOL_HALO:POOL_HALO + POOL_CHUNK].astype(F32)
        d = wsum / cnt - centre
        y = jnp.dot(d.astype(BF16), w_ref[0], preferred_element_type=F32) * sc_ref[...]
        o_ref[pl.ds(base, POOL_CHUNK), :] = y.astype(o_ref.dtype)
        return carry

    lax.fori_loop(0, seq // POOL_CHUNK, chunk, 0)


def _pool(z, bands, w_grp, scale, batch, seq, col_block0):
    n_g = len(POOL_WINDOWS)
    return pl.pallas_call(
        functools.partial(_pool_kernel, seq=seq),
        grid=(batch, n_g),
        in_specs=[pl.BlockSpec((seq, POOL_GROUP), lambda b, g: (b, col_block0 + g)),
                  pl.BlockSpec((1, POOL_CHUNK, POOL_CHUNK + 2 * POOL_HALO), lambda b, g: (g, 0, 0)),
                  pl.BlockSpec((1, POOL_GROUP, POOL_GROUP), lambda b, g: (g, 0, 0)),
                  pl.BlockSpec((1, POOL_GROUP), lambda b, g: (0, g))],
        out_specs=pl.BlockSpec((seq, POOL_GROUP), lambda b, g: (b, g)),
        out_shape=jax.ShapeDtypeStruct((batch * seq, POOL_W), BF16),
        scratch_shapes=[pltpu.VMEM((seq + 2 * POOL_HALO, POOL_GROUP), BF16)],
        compiler_params=_params("parallel", "arbitrary"),
        name="multiscale_pool",
    )(z, bands, w_grp, scale.reshape(1, POOL_W))


def _gelu_tanh(x):
    return 0.5 * x * (1.0 + jnp.tanh(0.7978845608028654 * (x + 0.044715 * (x * x * x))))


def _gmlp_kernel(u_ref, v_ref, lng_ref, lnb_ref, ws_ref, bs_ref, o_ref):
    v = _gelu_tanh(v_ref[...].astype(F32))
    mu = jnp.mean(v, axis=-1, keepdims=True)
    vc = v - mu
    var = jnp.mean(vc * vc, axis=-1, keepdims=True)
    vn = (vc * lax.rsqrt(var + EPS) * lng_ref[...] + lnb_ref[...]).astype(BF16)
    n_chunks = u_ref.shape[0] // GMLP_CHUNK
    for c in range(n_chunks):
        r = slice(c * GMLP_CHUNK, (c + 1) * GMLP_CHUNK)
        for g in range(GMLP_GROUPS):
            cs = slice(g * GMLP_GROUP_W, (g + 1) * GMLP_GROUP_W)
            mixed = jnp.dot(ws_ref[g], vn[r, cs], preferred_element_type=F32) + bs_ref[g]
            u = _gelu_tanh(u_ref[r, cs].astype(F32))
            o_ref[r, cs] = (u * mixed).astype(o_ref.dtype)


def _gmlp(z, ln_g, ln_b, w_s, b_s, u_block, tm=512):
    m = z.shape[0]
    return pl.pallas_call(
        _gmlp_kernel,
        grid=(m // tm,),
        in_specs=[pl.BlockSpec((tm, GMLP_W), lambda i: (i, u_block)),
                  pl.BlockSpec((tm, GMLP_W), lambda i: (i, u_block + 1)),
                  pl.BlockSpec((1, GMLP_W), lambda i: (0, 0)),
                  pl.BlockSpec((1, GMLP_W), lambda i: (0, 0)),
                  pl.BlockSpec((GMLP_GROUPS, GMLP_CHUNK, GMLP_CHUNK), lambda i: (0, 0, 0)),
                  pl.BlockSpec((GMLP_GROUPS, GMLP_CHUNK, 1), lambda i: (0, 0, 0))],
        out_specs=pl.BlockSpec((tm, GMLP_W), lambda i: (i, 0)),
        out_shape=jax.ShapeDtypeStruct((m, GMLP_W), BF16),
        compiler_params=_params("parallel"),
        name="spatial_gating",
    )(z, z, ln_g.reshape(1, GMLP_W), ln_b.reshape(1, GMLP_W), w_s, b_s.reshape(GMLP_GROUPS, GMLP_CHUNK, 1))


def _sigmoid(x):
    return 1.0 / (1.0 + jnp.exp(-x))


def _merge_kernel(ya_ref, yp_ref, ys_ref, hd_ref, wb_ref, gu0_ref, gu1_ref, gu2_ref,
                  gb0_ref, gb1_ref, gb2_ref, o_ref, wbb_ref, gub_ref):
    _cast_on_first_m_step([(wb_ref, wbb_ref), (gu0_ref, gub_ref.at[0]), (gu1_ref, gub_ref.at[1]),
                           (gu2_ref, gub_ref.at[2])])
    hd = hd_ref[...]
    o_pool = ATTN_W
    o_sg = ATTN_W + POOL_W
    ga = _sigmoid(jnp.dot(hd, gub_ref[0], preferred_element_type=F32) + gb0_ref[...])
    acc = ga * jnp.dot(ya_ref[...], wbb_ref[pl.ds(0, ATTN_W), :], preferred_element_type=F32)
    gp = _sigmoid(jnp.dot(hd, gub_ref[1], preferred_element_type=F32) + gb1_ref[...])
    acc = acc + gp * jnp.dot(yp_ref[...], wbb_ref[pl.ds(o_pool, POOL_W), :], preferred_element_type=F32)
    gs = _sigmoid(jnp.dot(hd, gub_ref[2], preferred_element_type=F32) + gb2_ref[...])
    acc = acc + gs * jnp.dot(ys_ref[...], wbb_ref[pl.ds(o_sg, GMLP_W), :], preferred_element_type=F32)
    o_ref[...] = acc.astype(o_ref.dtype)


def _merge(ya, yp, ys, hd, wb_stack, gate_up_stack, gate_b, layer, *, tm, tn):
    m = ya.shape[0]
    mix_w, d = wb_stack.shape[1:]
    rank = hd.shape[1]
    nb = d // tn
    row = lambda w: pl.BlockSpec((tm, w), lambda j, i: (i, 0))
    gate_b = gate_b.reshape(1, N_BRANCH * d)
    gu = lambda br: pl.BlockSpec((None, rank, tn), lambda j, i: (layer, 0, br * nb + j))
    gb = lambda br: pl.BlockSpec((1, tn), lambda j, i: (0, br * nb + j))
    return pl.pallas_call(
        _merge_kernel,
        grid=(nb, m // tm),
        in_specs=[row(ATTN_W), row(POOL_W), row(GMLP_W), row(rank),
                  pl.BlockSpec((None, mix_w, tn), lambda j, i: (layer, 0, j)),
                  gu(0), gu(1), gu(2), gb(0), gb(1), gb(2)],
        out_specs=pl.BlockSpec((tm, tn), lambda j, i: (i, j)),
        out_shape=jax.ShapeDtypeStruct((m, d), BF16),
        scratch_shapes=[pltpu.VMEM((mix_w, tn), BF16), pltpu.VMEM((N_BRANCH, rank, tn), BF16)],
        compiler_params=_params("arbitrary", "arbitrary"),
        name="branch_merge",
    )(ya, yp, ys, hd, wb_stack, gate_up_stack, gate_up_stack, gate_up_stack, gate_b, gate_b, gate_b)


def kernel(x, attn_norm_g, w_in, rpb, pool_w, pool_scale, gmlp_ln_g, gmlp_ln_b, gmlp_w_s, gmlp_b_s,
           w_branch, gate_down, gate_up, gate_b, w_out, ffn_norm_g, w_ffn_gate, w_ffn_up, w_ffn_down,
           final_norm_g):
    batch, seq, d = x.shape
    depth = w_in.shape[0]
    rows = seq // GRID_W
    xf = x.reshape(batch * seq, d)
    bands = _pool_band_matrices()
    pool_col_block = (3 * ATTN_W) // POOL_GROUP
    u_block = (3 * ATTN_W + POOL_W) // GMLP_W
    bias_src = _attn_bias_source(rpb)
    for l in range(depth):
        h = _rmsnorm(xf, attn_norm_g[l], BF16)
        z = _pw_call(_pw_mm_kernel, h, [w_in], l, tm=1024, tn=1024, out_dtype=BF16, name="in_proj")
        hd = _matmul_ws(h, gate_down, l, tm=1024, tn=gate_down.shape[2], out_dtype=BF16, name="gate_down")
        y_attn = _attention(z, bias_src, l, batch, seq)
        y_pool = _pool(z, bands, pool_w[l].astype(BF16), pool_scale[l], batch, seq, pool_col_block)
        y_sg = _gmlp(z, gmlp_ln_g[l], gmlp_ln_b[l], gmlp_w_s[l].astype(BF16), gmlp_b_s[l], u_block)
        merged = _merge(y_attn, y_pool, y_sg, hd, w_branch, gate_up, gate_b[l], l, tm=1024, tn=512)
        xf = _pw_call(_pw_mm_res_kernel, merged, [w_out], l, tm=1024, tn=512, out_dtype=F32, residual=xf,
                      name="out_proj")
        h = _rmsnorm(xf, ffn_norm_g[l], BF16)
        act = _pw_call(_pw_swiglu_kernel, h, [w_ffn_gate, w_ffn_up], l, tm=2048, tn=256, out_dtype=BF16,
                       name="swiglu")
        xf = _pw_call(_pw_mm_res_kernel, act, [w_ffn_down], l, tm=512, tn=512, out_dtype=F32, residual=xf,
                      name="ffn_down")
    out = _rmsnorm(xf, final_norm_g, F32)
    return out.reshape(batch, seq, d)
```

```python
import functools

import jax
import jax.numpy as jnp
from jax import lax
from jax.experimental import pallas as pl
from jax.experimental.pallas import tpu as pltpu

F32 = jnp.float32
BF16 = jnp.bfloat16

GRID_W = 64
N_HEADS = 16
HEAD_DIM = 128
ATTN_W = N_HEADS * HEAD_DIM
WIN_R = 8
WIN_C = 16
POOL_WINDOWS = (2, 4, 8, 16)
POOL_GROUP = 256
POOL_W = POOL_GROUP * len(POOL_WINDOWS)
GMLP_CHUNK = 128
GMLP_GROUPS = 4
GMLP_GROUP_W = 256
GMLP_W = GMLP_GROUPS * GMLP_GROUP_W
N_BRANCH = 3
EPS = 1e-6

VMEM_LIMIT_BYTES = 56 * 1024 * 1024
LANE = 128

ATTN_Q_ROWS = 4
ATTN_K_ROWS = ATTN_Q_ROWS + WIN_R - 1
POOL_CHUNK = 256
POOL_HALO = 128
MASKED = -1e30


def _params(*sem):
    return pltpu.CompilerParams(dimension_semantics=sem, vmem_limit_bytes=VMEM_LIMIT_BYTES)


def _rmsnorm_kernel(x_ref, g_ref, o_ref):
    x = x_ref[...]
    ms = jnp.mean(x * x, axis=-1, keepdims=True)
    o_ref[...] = (x * lax.rsqrt(ms + EPS) * g_ref[...]).astype(o_ref.dtype)


def _rmsnorm(x, g, out_dtype, tm=256):
    m, d = x.shape
    return pl.pallas_call(
        _rmsnorm_kernel,
        grid=(m // tm,),
        in_specs=[pl.BlockSpec((tm, d), lambda i: (i, 0)),
                  pl.BlockSpec((1, d), lambda i: (0, 0))],
        out_specs=pl.BlockSpec((tm, d), lambda i: (i, 0)),
        out_shape=jax.ShapeDtypeStruct((m, d), out_dtype),
        compiler_params=_params("parallel"),
        name="rmsnorm",
    )(x, g.reshape(1, d))


def _prenorm_kernel(x_ref, g_ref, hu_ref, rs_ref):
    x = x_ref[...]
    ms = jnp.mean(x * x, axis=-1, keepdims=True)
    hu_ref[...] = (x * g_ref[...]).astype(hu_ref.dtype)
    rs_ref[...] = jnp.broadcast_to(lax.rsqrt(ms + EPS), rs_ref.shape)


def _prenorm(x, g, tm=256):
    m, d = x.shape
    return pl.pallas_call(
        _prenorm_kernel,
        grid=(m // tm,),
        in_specs=[pl.BlockSpec((tm, d), lambda i: (i, 0)),
                  pl.BlockSpec((1, d), lambda i: (0, 0))],
        out_specs=[pl.BlockSpec((tm, d), lambda i: (i, 0)),
                   pl.BlockSpec((tm, LANE), lambda i: (i, 0))],
        out_shape=[jax.ShapeDtypeStruct((m, d), BF16), jax.ShapeDtypeStruct((m, LANE), F32)],
        compiler_params=_params("parallel"),
        name="prenorm",
    )(x, g.reshape(1, d))


def _row_scale_kernel(ssq_ref, rs_ref, *, d):
    rs_ref[...] = lax.rsqrt(jnp.sum(ssq_ref[...], axis=0) * (1.0 / d) + EPS)


def _row_scale(ssq_parts, d, tm=1024):
    n_t, m, _ = ssq_parts.shape
    return pl.pallas_call(
        functools.partial(_row_scale_kernel, d=d),
        grid=(m // tm,),
        in_specs=[pl.BlockSpec((n_t, tm, LANE), lambda i: (0, i, 0))],
        out_specs=pl.BlockSpec((tm, LANE), lambda i: (i, 0)),
        out_shape=jax.ShapeDtypeStruct((m, LANE), F32),
        compiler_params=_params("parallel"),
        name="row_scale",
    )(ssq_parts)


def _cast_on_first_m_step(pairs):
    @pl.when(pl.program_id(1) == 0)
    def _():
        for src, dst in pairs:
            dst[...] = src[...].astype(BF16)


def _ws_mm_kernel(a_ref, w_ref, rs_ref, o_ref, wb_ref):
    _cast_on_first_m_step([(w_ref, wb_ref)])
    acc = jnp.dot(a_ref[...], wb_ref[...], preferred_element_type=F32)
    o_ref[...] = (acc * rs_ref[:, :1]).astype(o_ref.dtype)


def _matmul_ws(a, w_stack, layer, row_scale, *, tm, tn, out_dtype, name):
    m, k = a.shape
    n = w_stack.shape[2]
    return pl.pallas_call(
        _ws_mm_kernel,
        grid=(n // tn, m // tm),
        in_specs=[pl.BlockSpec((tm, k), lambda j, i: (i, 0)),
                  pl.BlockSpec((None, k, tn), lambda j, i: (layer, 0, j)),
                  pl.BlockSpec((tm, LANE), lambda j, i: (i, 0))],
        out_specs=pl.BlockSpec((tm, tn), lambda j, i: (i, j)),
        out_shape=jax.ShapeDtypeStruct((m, n), out_dtype),
        scratch_shapes=[pltpu.VMEM((k, tn), BF16)],
        compiler_params=_params("arbitrary", "arbitrary"),
        name=name,
    )(a, w_stack, row_scale)


def _pw_fill(w_refs, slot_refs, rows):
    r0 = pl.multiple_of(pl.program_id(1) * rows, rows)
    for w_ref, slot_ref in zip(w_refs, slot_refs):
        slot_ref[pl.ds(r0, rows), :] = w_ref[...].astype(BF16)


def _pw_phases(w_refs, even_slots, odd_slots, rows, compute):
    j = pl.program_id(0)

    @pl.when(j == 0)
    def _():
        _pw_fill(w_refs, even_slots, rows)

    @pl.when((j > 0) & (j % 2 == 1))
    def _():
        _pw_fill(w_refs, odd_slots, rows)
        compute(even_slots)

    @pl.when((j > 0) & (j % 2 == 0))
    def _():
        _pw_fill(w_refs, even_slots, rows)
        compute(odd_slots)


def _pw_mm_kernel(a_ref, w_ref, rs_ref, o_ref, w0_ref, w1_ref, *, rows):
    def compute(slots):
        acc = jnp.dot(a_ref[...], slots[0][...], preferred_element_type=F32)
        o_ref[...] = (acc * rs_ref[:, :1]).astype(o_ref.dtype)
    _pw_phases([w_ref], [w0_ref], [w1_ref], rows, compute)


def _pw_mm_res_kernel(a_ref, w_ref, r_ref, o_ref, w0_ref, w1_ref, *, rows):
    def compute(slots):
        o_ref[...] = r_ref[...] + jnp.dot(a_ref[...], slots[0][...], preferred_element_type=F32)
    _pw_phases([w_ref], [w0_ref], [w1_ref], rows, compute)


def _pw_mm_res_norm_kernel(a_ref, w_ref, r_ref, g_ref, o_ref, hu_ref, ssq_ref, w0_ref, w1_ref, *, rows):
    def compute(slots):
        x = r_ref[...] + jnp.dot(a_ref[...], slots[0][...], preferred_element_type=F32)
        o_ref[...] = x
        hu_ref[...] = (x * g_ref[...]).astype(hu_ref.dtype)
        ssq_ref[...] = jnp.broadcast_to(jnp.sum(x * x, axis=-1, keepdims=True), ssq_ref.shape)
    _pw_phases([w_ref], [w0_ref], [w1_ref], rows, compute)


def _pw_swiglu_kernel(h_ref, wg_ref, wu_ref, rs_ref, o_ref, g0_ref, u0_ref, g1_ref, u1_ref, *, rows):
    def compute(slots):
        h = h_ref[...]
        rs = rs_ref[:, :1]
        a = jnp.dot(h, slots[0][...], preferred_element_type=F32) * rs
        b = jnp.dot(h, slots[1][...], preferred_element_type=F32) * rs
        o_ref[...] = (a * (1.0 / (1.0 + jnp.exp(-a))) * b).astype(o_ref.dtype)
    _pw_phases([wg_ref, wu_ref], [g0_ref, u0_ref], [g1_ref, u1_ref], rows, compute)


def _pw_call(kern, a, w_stacks, layer, *, tm, tn, out_dtype, name, row_scale=None, residual=None,
             next_gain=None):
    m, k = a.shape
    n = w_stacks[0].shape[2]
    n_n, n_m = n // tn, m // tm
    rows = k // n_m
    assert k % n_m == 0 and rows % 16 == 0, (k, n_m)
    m_idx = lambda j, i: jnp.where(j == 0, 0, i)
    n_idx = lambda j: jnp.maximum(j - 1, 0)
    w_spec = pl.BlockSpec((None, rows, tn),
                          lambda j, i: (layer, jnp.where(j < n_n, i, n_m - 1), jnp.minimum(j, n_n - 1)))
    o_spec = pl.BlockSpec((tm, tn), lambda j, i: (m_idx(j, i), n_idx(j)))
    in_specs = [pl.BlockSpec((tm, k), lambda j, i: (m_idx(j, i), 0))] + [w_spec] * len(w_stacks)
    args = [a, *w_stacks]
    out_specs = o_spec
    out_shape = jax.ShapeDtypeStruct((m, n), out_dtype)
    if row_scale is not None:
        in_specs.append(pl.BlockSpec((tm, LANE), lambda j, i: (m_idx(j, i), 0)))
        args.append(row_scale)
    if residual is not None:
        in_specs.append(o_spec)
        args.append(residual)
    if next_gain is not None:
        in_specs.append(pl.BlockSpec((1, tn), lambda j, i: (0, n_idx(j))))
        args.append(next_gain.reshape(1, n))
        out_specs = [o_spec, o_spec, pl.BlockSpec((None, tm, LANE), lambda j, i: (n_idx(j), m_idx(j, i), 0))]
        out_shape = [out_shape, jax.ShapeDtypeStruct((m, n), BF16),
                     jax.ShapeDtypeStruct((n_n, m, LANE), F32)]
    return pl.pallas_call(
        functools.partial(kern, rows=rows),
        grid=(n_n + 1, n_m),
        in_specs=in_specs,
        out_specs=out_specs,
        out_shape=out_shape,
        scratch_shapes=[pltpu.VMEM((k, tn), BF16)] * (2 * len(w_stacks)),
        compiler_params=_params("arbitrary", "arbitrary"),
        name=name,
    )(*args)


def _attn_group_start(g, rows):
    return min(max(g * ATTN_Q_ROWS - WIN_R // 2, 0), rows - ATTN_K_ROWS)


def _attn_group_type(g, n_groups):
    return 0 if g == 0 else (2 if g == n_groups - 1 else 1)


ATTN_SRC_LEFT = ATTN_Q_ROWS - 1
ATTN_SRC_W = -(-((WIN_R - 1 + ATTN_SRC_LEFT + ATTN_K_ROWS) * GRID_W) // LANE) * LANE


def _attn_bias_source(rpb):
    nl, h, n_ro, _ = rpb.shape
    rpb = rpb.astype(F32) * (HEAD_DIM ** 0.5)
    qc = jnp.arange(GRID_W)[:, None]
    kc = jnp.arange(GRID_W)[None, :]
    co = jnp.clip(kc - qc, -(WIN_C - 1), WIN_C - 1) + (WIN_C - 1)
    pick = (jnp.arange(2 * WIN_C - 1)[:, None, None] == co[None]).astype(F32)
    toep = jnp.einsum("lhrk,kqc->lhqrc", rpb, pick, precision=lax.Precision.HIGHEST)
    cs = jnp.clip(qc - WIN_C // 2, 0, GRID_W - WIN_C)
    valid = (kc >= cs) & (kc < cs + WIN_C)
    toep = jnp.where(valid[:, None, :], toep, MASKED)
    by_q = toep.reshape(nl, h, GRID_W, n_ro * GRID_W)
    copies = []
    for shift in (0, 1):
        left = (ATTN_SRC_LEFT - shift) * GRID_W
        pad = ((0, 0), (0, 0), (0, 0), (left, ATTN_SRC_W - left - n_ro * GRID_W))
        copies.append(jnp.pad(by_q, pad, constant_values=MASKED))
    return jnp.stack(copies, axis=2)


def _attn_fill_tables(src_ref, tab_ref, rows):
    n_groups = rows // ATTN_Q_ROWS
    nk = ATTN_K_ROWS * GRID_W
    lane = lax.broadcasted_iota(jnp.int32, (GRID_W, nk), 1)
    for t, g in enumerate((0, 1, n_groups - 1)):
        start = _attn_group_start(g, rows)
        for i in range(ATTN_Q_ROWS):
            qr = g * ATTN_Q_ROWS + i
            rs = min(max(qr - WIN_R // 2, 0), rows - WIN_R)
            lo = (rs - start) * GRID_W
            u = start - qr + (WIN_R - 1) + ATTN_SRC_LEFT
            copy = u % 2
            window = src_ref[copy, :, pl.ds((u - copy) * GRID_W, nk)]
            valid = (lane >= lo) & (lane < lo + WIN_R * GRID_W)
            tab_ref[t, pl.ds(i * GRID_W, GRID_W), :] = jnp.where(valid, window, MASKED)


def _attn_kernel(q_ref, k_ref, v_ref, src_ref, o_ref, tab_ref, *, rows):
    n_groups = rows // ATTN_Q_ROWS
    tq = ATTN_Q_ROWS * GRID_W
    nk = ATTN_K_ROWS * GRID_W
    exp2_scale = (HEAD_DIM ** -0.5) * 1.4426950408889634
    _attn_fill_tables(src_ref, tab_ref, rows)
    for g in range(n_groups):
        start = _attn_group_start(g, rows) * GRID_W
        q = q_ref[pl.ds(g * tq, tq), :]
        kw = k_ref[pl.ds(start, nk), :]
        vw = v_ref[pl.ds(start, nk), :]
        s = lax.dot_general(q, kw, (((1,), (1,)), ((), ())), preferred_element_type=F32)
        s = s + tab_ref[_attn_group_type(g, n_groups)]
        p = jnp.exp2((s - jnp.max(s, axis=-1, keepdims=True)) * exp2_scale)
        l = jnp.sum(p, axis=-1, keepdims=True)
        o = jnp.dot(p.astype(BF16), vw, preferred_element_type=F32)
        o_ref[pl.ds(g * tq, tq), :] = (o / l).astype(o_ref.dtype)


def _attention(z, bias_src, layer, batch, seq):
    rows = seq // GRID_W
    tq = ATTN_Q_ROWS * GRID_W
    tk = ATTN_K_ROWS * GRID_W
    return pl.pallas_call(
        functools.partial(_attn_kernel, rows=rows),
        grid=(batch, N_HEADS),
        in_specs=[pl.BlockSpec((seq, HEAD_DIM), lambda b, h: (b, h)),
                  pl.BlockSpec((seq, HEAD_DIM), lambda b, h: (b, N_HEADS + h)),
                  pl.BlockSpec((seq, HEAD_DIM), lambda b, h: (b, 2 * N_HEADS + h)),
                  pl.BlockSpec((None, None, 2, GRID_W, ATTN_SRC_W), lambda b, h: (layer, h, 0, 0, 0))],
        out_specs=pl.BlockSpec((seq, HEAD_DIM), lambda b, h: (b, h)),
        out_shape=jax.ShapeDtypeStruct((batch * seq, ATTN_W), BF16),
        scratch_shapes=[pltpu.VMEM((3, tq, tk), F32)],
        compiler_params=_params("parallel", "parallel"),
        name="nbr_attention",
    )(z, z, z, bias_src)


def _pool_band_matrices():
    i = jnp.arange(POOL_CHUNK)[:, None]
    off = jnp.arange(POOL_CHUNK + 2 * POOL_HALO)[None, :] - POOL_HALO - i
    return jnp.stack([((off >= -(w // 2)) & (off <= w // 2 - 1)).astype(BF16) for w in POOL_WINDOWS])


def _pool_kernel(p_ref, band_ref, w_ref, sc_ref, o_ref, pad_ref, *, seq):
    g = pl.program_id(1)
    half = jnp.left_shift(1, g)
    cols = p_ref.shape[1]
    pad_ref[pl.ds(0, POOL_HALO), :] = jnp.zeros((POOL_HALO, cols), BF16)
    pad_ref[pl.ds(POOL_HALO + seq, POOL_HALO), :] = jnp.zeros((POOL_HALO, cols), BF16)
    pad_ref[pl.ds(POOL_HALO, seq), :] = p_ref[...]

    def chunk(c, carry):
        base = pl.multiple_of(c * POOL_CHUNK, POOL_CHUNK)
        ph = pad_ref[pl.ds(base, POOL_CHUNK + 2 * POOL_HALO), :]
        wsum = jnp.dot(band_ref[0], ph, preferred_element_type=F32)
        t = base + lax.broadcasted_iota(jnp.int32, (POOL_CHUNK, 1), 0)
        lo = jnp.maximum(t - half, 0)
        hi = jnp.minimum(t + half - 1, seq - 1)
        cnt = (hi - lo + 1).astype(F32)
        centre = ph[POOL_HALO:POOL_HALO + POOL_CHUNK].astype(F32)
        d = wsum / cnt - centre
        y = jnp.dot(d.astype(BF16), w_ref[0], preferred_element_type=F32) * sc_ref[...]
        o_ref[pl.ds(base, POOL_CHUNK), :] = y.astype(o_ref.dtype)
        return carry

    lax.fori_loop(0, seq // POOL_CHUNK, chunk, 0)


def _pool(z, bands, w_grp, scale, batch, seq, col_block0):
    n_g = len(POOL_WINDOWS)
    return pl.pallas_call(
        functools.partial(_pool_kernel, seq=seq),
        grid=(batch, n_g),
        in_specs=[pl.BlockSpec((seq, POOL_GROUP), lambda b, g: (b, col_block0 + g)),
                  pl.BlockSpec((1, POOL_CHUNK, POOL_CHUNK + 2 * POOL_HALO), lambda b, g: (g, 0, 0)),
                  pl.BlockSpec((1, POOL_GROUP, POOL_GROUP), lambda b, g: (g, 0, 0)),
                  pl.BlockSpec((1, POOL_GROUP), lambda b, g: (0, g))],
        out_specs=pl.BlockSpec((seq, POOL_GROUP), lambda b, g: (b, g)),
        out_shape=jax.ShapeDtypeStruct((batch * seq, POOL_W), BF16),
        scratch_shapes=[pltpu.VMEM((seq + 2 * POOL_HALO, POOL_GROUP), BF16)],
        compiler_params=_params("parallel", "arbitrary"),
        name="multiscale_pool",
    )(z, bands, w_grp, scale.reshape(1, POOL_W))


def _gelu_tanh(x):
    return 0.5 * x * (1.0 + jnp.tanh(0.7978845608028654 * (x + 0.044715 * (x * x * x))))


def _gmlp_kernel(u_ref, v_ref, lng_ref, lnb_ref, ws_ref, bs_ref, o_ref):
    v = _gelu_tanh(v_ref[...].astype(F32))
    mu = jnp.mean(v, axis=-1, keepdims=True)
    vc = v - mu
    var = jnp.mean(vc * vc, axis=-1, keepdims=True)
    vn = (vc * lax.rsqrt(var + EPS) * lng_ref[...] + lnb_ref[...]).astype(BF16)
    n_chunks = u_ref.shape[0] // GMLP_CHUNK
    for c in range(n_chunks):
        r = slice(c * GMLP_CHUNK, (c + 1) * GMLP_CHUNK)
        for g in range(GMLP_GROUPS):
            cs = slice(g * GMLP_GROUP_W, (g + 1) * GMLP_GROUP_W)
            mixed = jnp.dot(ws_ref[g], vn[r, cs], preferred_element_type=F32) + bs_ref[g]
            u = _gelu_tanh(u_ref[r, cs].astype(F32))
            o_ref[r, cs] = (u * mixed).astype(o_ref.dtype)


def _gmlp(z, ln_g, ln_b, w_s, b_s, u_block, tm=512):
    m = z.shape[0]
    return pl.pallas_call(
        _gmlp_kernel,
        grid=(m // tm,),
        in_specs=[pl.BlockSpec((tm, GMLP_W), lambda i: (i, u_block)),
                  pl.BlockSpec((tm, GMLP_W), lambda i: (i, u_block + 1)),
                  pl.BlockSpec((1, GMLP_W), lambda i: (0, 0)),
                  pl.BlockSpec((1, GMLP_W), lambda i: (0, 0)),
                  pl.BlockSpec((GMLP_GROUPS, GMLP_CHUNK, GMLP_CHUNK), lambda i: (0, 0, 0)),
                  pl.BlockSpec((GMLP_GROUPS, GMLP_CHUNK, 1), lambda i: (0, 0, 0))],
        out_specs=pl.BlockSpec((tm, GMLP_W), lambda i: (i, 0)),
        out_shape=jax.ShapeDtypeStruct((m, GMLP_W), BF16),
        compiler_params=_params("parallel"),
        name="spatial_gating",
    )(z, z, ln_g.reshape(1, GMLP_W), ln_b.reshape(1, GMLP_W), w_s, b_s.reshape(GMLP_GROUPS, GMLP_CHUNK, 1))


def _sigmoid(x):
    return 1.0 / (1.0 + jnp.exp(-x))


def _merge_kernel(ya_ref, yp_ref, ys_ref, hd_ref, wb_ref, gu0_ref, gu1_ref, gu2_ref,
                  gb0_ref, gb1_ref, gb2_ref, o_ref, wbb_ref, gub_ref):
    _cast_on_first_m_step([(wb_ref, wbb_ref), (gu0_ref, gub_ref.at[0]), (gu1_ref, gub_ref.at[1]),
                           (gu2_ref, gub_ref.at[2])])
    hd = hd_ref[...]
    o_pool = ATTN_W
    o_sg = ATTN_W + POOL_W
    ga = _sigmoid(jnp.dot(hd, gub_ref[0], preferred_element_type=F32) + gb0_ref[...])
    acc = ga * jnp.dot(ya_ref[...], wbb_ref[pl.ds(0, ATTN_W), :], preferred_element_type=F32)
    gp = _sigmoid(jnp.dot(hd, gub_ref[1], preferred_element_type=F32) + gb1_ref[...])
    acc = acc + gp * jnp.dot(yp_ref[...], wbb_ref[pl.ds(o_pool, POOL_W), :], preferred_element_type=F32)
    gs = _sigmoid(jnp.dot(hd, gub_ref[2], preferred_element_type=F32) + gb2_ref[...])
    acc = acc + gs * jnp.dot(ys_ref[...], wbb_ref[pl.ds(o_sg, GMLP_W), :], preferred_element_type=F32)
    o_ref[...] = acc.astype(o_ref.dtype)


def _merge(ya, yp, ys, hd, wb_stack, gate_up_stack, gate_b, layer, *, tm, tn):
    m = ya.shape[0]
    mix_w, d = wb_stack.shape[1:]
    rank = hd.shape[1]
    nb = d // tn
    row = lambda w: pl.BlockSpec((tm, w), lambda j, i: (i, 0))
    gate_b = gate_b.reshape(1, N_BRANCH * d)
    gu = lambda br: pl.BlockSpec((None, rank, tn), lambda j, i: (layer, 0, br * nb + j))
    gb = lambda br: pl.BlockSpec((1, tn), lambda j, i: (0, br * nb + j))
    return pl.pallas_call(
        _merge_kernel,
        grid=(nb, m // tm),
        in_specs=[row(ATTN_W), row(POOL_W), row(GMLP_W), row(rank),
                  pl.BlockSpec((None, mix_w, tn), lambda j, i: (layer, 0, j)),
                  gu(0), gu(1), gu(2), gb(0), gb(1), gb(2)],
        out_specs=pl.BlockSpec((tm, tn), lambda j, i: (i, j)),
        out_shape=jax.ShapeDtypeStruct((m, d), BF16),
        scratch_shapes=[pltpu.VMEM((mix_w, tn), BF16), pltpu.VMEM((N_BRANCH, rank, tn), BF16)],
        compiler_params=_params("arbitrary", "arbitrary"),
        name="branch_merge",
    )(ya, yp, ys, hd, wb_stack, gate_up_stack, gate_up_stack, gate_up_stack, gate_b, gate_b, gate_b)


def kernel(x, attn_norm_g, w_in, rpb, pool_w, pool_scale, gmlp_ln_g, gmlp_ln_b, gmlp_w_s, gmlp_b_s,
           w_branch, gate_down, gate_up, gate_b, w_out, ffn_norm_g, w_ffn_gate, w_ffn_up, w_ffn_down,
           final_norm_g):
    batch, seq, d = x.shape
    depth = w_in.shape[0]
    xf = x.reshape(batch * seq, d)
    bands = _pool_band_matrices()
    pool_col_block = (3 * ATTN_W) // POOL_GROUP
    u_block = (3 * ATTN_W + POOL_W) // GMLP_W
    bias_src = _attn_bias_source(rpb)
    hu, rs = _prenorm(xf, attn_norm_g[0])
    for l in range(depth):
        z = _pw_call(_pw_mm_kernel, hu, [w_in], l, tm=1024, tn=1024, out_dtype=BF16, row_scale=rs,
                     name="in_proj")
        hd = _matmul_ws(hu, gate_down, l, rs, tm=1024, tn=gate_down.shape[2], out_dtype=BF16,
                        name="gate_down")
        y_attn = _attention(z, bias_src, l, batch, seq)
        y_pool = _pool(z, bands, pool_w[l].astype(BF16), pool_scale[l], batch, seq, pool_col_block)
        y_sg = _gmlp(z, gmlp_ln_g[l], gmlp_ln_b[l], gmlp_w_s[l].astype(BF16), gmlp_b_s[l], u_block)
        merged = _merge(y_attn, y_pool, y_sg, hd, w_branch, gate_up, gate_b[l], l, tm=1024, tn=512)
        xf, hu, ssq = _pw_call(_pw_mm_res_norm_kernel, merged, [w_out], l, tm=1024, tn=512, out_dtype=F32,
                               residual=xf, next_gain=ffn_norm_g[l], name="out_proj")
        rs = _row_scale(ssq, d)
        act = _pw_call(_pw_swiglu_kernel, hu, [w_ffn_gate, w_ffn_up], l, tm=2048, tn=256, out_dtype=BF16,
                       row_scale=rs, name="swiglu")
        if l + 1 < depth:
            xf, hu, ssq = _pw_call(_pw_mm_res_norm_kernel, act, [w_ffn_down], l, tm=512, tn=512,
                                   out_dtype=F32, residual=xf, next_gain=attn_norm_g[l + 1],
                                   name="ffn_down")
            rs = _row_scale(ssq, d)
        else:
            xf = _pw_call(_pw_mm_res_kernel, act, [w_ffn_down], l, tm=512, tn=512, out_dtype=F32,
                          residual=xf, name="ffn_down")
    out = _rmsnorm(xf, final_norm_g, F32)
    return out.reshape(batch, seq, d)
```

```python
import functools

import jax
import jax.numpy as jnp
from jax import lax
from jax.experimental import pallas as pl
from jax.experimental.pallas import tpu as pltpu

F32 = jnp.float32
BF16 = jnp.bfloat16

GRID_W = 64
N_HEADS = 16
HEAD_DIM = 128
ATTN_W = N_HEADS * HEAD_DIM
WIN_R = 8
WIN_C = 16
POOL_WINDOWS = (2, 4, 8, 16)
POOL_GROUP = 256
POOL_W = POOL_GROUP * len(POOL_WINDOWS)
GMLP_CHUNK = 128
GMLP_GROUPS = 4
GMLP_GROUP_W = 256
GMLP_W = GMLP_GROUPS * GMLP_GROUP_W
N_BRANCH = 3
EPS = 1e-6

VMEM_LIMIT_BYTES = 60 * 1024 * 1024
LANE = 128

ATTN_Q_ROWS = 4
ATTN_K_ROWS = ATTN_Q_ROWS + WIN_R - 1
POOL_CHUNK = 256
POOL_HALO = 128
MASKED = -1e30


def _params(*sem):
    return pltpu.CompilerParams(dimension_semantics=sem, vmem_limit_bytes=VMEM_LIMIT_BYTES)


def _rmsnorm_kernel(x_ref, g_ref, o_ref):
    x = x_ref[...]
    ms = jnp.mean(x * x, axis=-1, keepdims=True)
    o_ref[...] = (x * lax.rsqrt(ms + EPS) * g_ref[...]).astype(o_ref.dtype)


def _rmsnorm(x, g, out_dtype, tm=256):
    m, d = x.shape
    return pl.pallas_call(
        _rmsnorm_kernel,
        grid=(m // tm,),
        in_specs=[pl.BlockSpec((tm, d), lambda i: (i, 0)),
                  pl.BlockSpec((1, d), lambda i: (0, 0))],
        out_specs=pl.BlockSpec((tm, d), lambda i: (i, 0)),
        out_shape=jax.ShapeDtypeStruct((m, d), out_dtype),
        compiler_params=_params("parallel"),
        name="rmsnorm",
    )(x, g.reshape(1, d))


def _prenorm_kernel(x_ref, g_ref, hu_ref, rs_ref):
    x = x_ref[...]
    ms = jnp.mean(x * x, axis=-1, keepdims=True)
    hu_ref[...] = (x * g_ref[...]).astype(hu_ref.dtype)
    rs_ref[...] = jnp.broadcast_to(lax.rsqrt(ms + EPS), rs_ref.shape)


def _prenorm(x, g, tm=256):
    m, d = x.shape
    return pl.pallas_call(
        _prenorm_kernel,
        grid=(m // tm,),
        in_specs=[pl.BlockSpec((tm, d), lambda i: (i, 0)),
                  pl.BlockSpec((1, d), lambda i: (0, 0))],
        out_specs=[pl.BlockSpec((tm, d), lambda i: (i, 0)),
                   pl.BlockSpec((tm, LANE), lambda i: (i, 0))],
        out_shape=[jax.ShapeDtypeStruct((m, d), BF16), jax.ShapeDtypeStruct((m, LANE), F32)],
        compiler_params=_params("parallel"),
        name="prenorm",
    )(x, g.reshape(1, d))


def _row_scale_kernel(ssq_ref, rs_ref, *, d):
    rs_ref[...] = lax.rsqrt(jnp.sum(ssq_ref[...], axis=0) * (1.0 / d) + EPS)


def _row_scale(ssq_parts, d, tm=1024):
    n_t, m, _ = ssq_parts.shape
    return pl.pallas_call(
        functools.partial(_row_scale_kernel, d=d),
        grid=(m // tm,),
        in_specs=[pl.BlockSpec((n_t, tm, LANE), lambda i: (0, i, 0))],
        out_specs=pl.BlockSpec((tm, LANE), lambda i: (i, 0)),
        out_shape=jax.ShapeDtypeStruct((m, LANE), F32),
        compiler_params=_params("parallel"),
        name="row_scale",
    )(ssq_parts)


def _cast_on_first_m_step(pairs):
    @pl.when(pl.program_id(1) == 0)
    def _():
        for src, dst in pairs:
            dst[...] = src[...].astype(BF16)


def _ws_mm_kernel(a_ref, w_ref, rs_ref, o_ref, wb_ref):
    _cast_on_first_m_step([(w_ref, wb_ref)])
    acc = jnp.dot(a_ref[...], wb_ref[...], preferred_element_type=F32)
    o_ref[...] = (acc * rs_ref[:, :1]).astype(o_ref.dtype)


def _matmul_ws(a, w_stack, layer, row_scale, *, tm, tn, out_dtype, name):
    m, k = a.shape
    n = w_stack.shape[2]
    return pl.pallas_call(
        _ws_mm_kernel,
        grid=(n // tn, m // tm),
        in_specs=[pl.BlockSpec((tm, k), lambda j, i: (i, 0)),
                  pl.BlockSpec((None, k, tn), lambda j, i: (layer, 0, j)),
                  pl.BlockSpec((tm, LANE), lambda j, i: (i, 0))],
        out_specs=pl.BlockSpec((tm, tn), lambda j, i: (i, j)),
        out_shape=jax.ShapeDtypeStruct((m, n), out_dtype),
        scratch_shapes=[pltpu.VMEM((k, tn), BF16)],
        compiler_params=_params("arbitrary", "arbitrary"),
        name=name,
    )(a, w_stack, row_scale)


def _pw_fill(w_refs, slot_refs):
    for w_ref, slot_ref in zip(w_refs, slot_refs):
        rows = w_ref.shape[0]
        r0 = pl.multiple_of(pl.program_id(1) * rows, rows)
        slot_ref[pl.ds(r0, rows), :] = w_ref[...].astype(BF16)


def _pw_phases(w_refs, even_slots, odd_slots, compute):
    j = pl.program_id(0)

    @pl.when(j == 0)
    def _():
        _pw_fill(w_refs, even_slots)

    @pl.when((j > 0) & (j % 2 == 1))
    def _():
        _pw_fill(w_refs, odd_slots)
        compute(even_slots)

    @pl.when((j > 0) & (j % 2 == 0))
    def _():
        _pw_fill(w_refs, even_slots)
        compute(odd_slots)


def _pw_mm_kernel(a_ref, w_ref, rs_ref, o_ref, w0_ref, w1_ref):
    def compute(slots):
        acc = jnp.dot(a_ref[...], slots[0][...], preferred_element_type=F32)
        o_ref[...] = (acc * rs_ref[:, :1]).astype(o_ref.dtype)
    _pw_phases([w_ref], [w0_ref], [w1_ref], compute)


def _pw_mm_res_kernel(a_ref, w_ref, r_ref, o_ref, w0_ref, w1_ref):
    def compute(slots):
        o_ref[...] = r_ref[...] + jnp.dot(a_ref[...], slots[0][...], preferred_element_type=F32)
    _pw_phases([w_ref], [w0_ref], [w1_ref], compute)


def _pw_mm_res_norm_kernel(a_ref, w_ref, r_ref, g_ref, o_ref, hu_ref, ssq_ref, w0_ref, w1_ref):
    def compute(slots):
        x = r_ref[...] + jnp.dot(a_ref[...], slots[0][...], preferred_element_type=F32)
        o_ref[...] = x
        hu_ref[...] = (x * g_ref[...]).astype(hu_ref.dtype)
        ssq_ref[...] = jnp.broadcast_to(jnp.sum(x * x, axis=-1, keepdims=True), ssq_ref.shape)
    _pw_phases([w_ref], [w0_ref], [w1_ref], compute)


def _pw_swiglu_kernel(h_ref, wg_ref, wu_ref, rs_ref, o_ref, g0_ref, u0_ref, g1_ref, u1_ref):
    def compute(slots):
        h = h_ref[...]
        rs = rs_ref[:, :1]
        a = jnp.dot(h, slots[0][...], preferred_element_type=F32) * rs
        b = jnp.dot(h, slots[1][...], preferred_element_type=F32) * rs
        o_ref[...] = (a * (1.0 / (1.0 + jnp.exp(-a))) * b).astype(o_ref.dtype)
    _pw_phases([wg_ref, wu_ref], [g0_ref, u0_ref], [g1_ref, u1_ref], compute)


def _pw_call(kern, a, w_stacks, layer, *, tm, tn, out_dtype, name, row_scale=None, residual=None,
             next_gain=None):
    m, k = a.shape
    n = w_stacks[0].shape[2]
    n_n, n_m = n // tn, m // tm
    rows = k // n_m
    assert k % n_m == 0 and rows % 16 == 0, (k, n_m)
    m_idx = lambda j, i: jnp.where(j == 0, 0, i)
    n_idx = lambda j: jnp.maximum(j - 1, 0)
    w_spec = pl.BlockSpec((None, rows, tn),
                          lambda j, i: (layer, jnp.where(j < n_n, i, n_m - 1), jnp.minimum(j, n_n - 1)))
    o_spec = pl.BlockSpec((tm, tn), lambda j, i: (m_idx(j, i), n_idx(j)))
    in_specs = [pl.BlockSpec((tm, k), lambda j, i: (m_idx(j, i), 0))] + [w_spec] * len(w_stacks)
    args = [a, *w_stacks]
    out_specs = o_spec
    out_shape = jax.ShapeDtypeStruct((m, n), out_dtype)
    if row_scale is not None:
        in_specs.append(pl.BlockSpec((tm, LANE), lambda j, i: (m_idx(j, i), 0)))
        args.append(row_scale)
    if residual is not None:
        in_specs.append(o_spec)
        args.append(residual)
    if next_gain is not None:
        in_specs.append(pl.BlockSpec((1, tn), lambda j, i: (0, n_idx(j))))
        args.append(next_gain.reshape(1, n))
        out_specs = [o_spec, o_spec, pl.BlockSpec((None, tm, LANE), lambda j, i: (n_idx(j), m_idx(j, i), 0))]
        out_shape = [out_shape, jax.ShapeDtypeStruct((m, n), BF16),
                     jax.ShapeDtypeStruct((n_n, m, LANE), F32)]
    return pl.pallas_call(
        kern,
        grid=(n_n + 1, n_m),
        in_specs=in_specs,
        out_specs=out_specs,
        out_shape=out_shape,
        scratch_shapes=[pltpu.VMEM((k, tn), BF16)] * (2 * len(w_stacks)),
        compiler_params=_params("arbitrary", "arbitrary"),
        name=name,
    )(*args)


def _attn_group_start(g, rows):
    return min(max(g * ATTN_Q_ROWS - WIN_R // 2, 0), rows - ATTN_K_ROWS)


def _attn_row_windows(g, rows):
    start = _attn_group_start(g, rows)
    out = []
    for i in range(ATTN_Q_ROWS):
        qr = g * ATTN_Q_ROWS + i
        rs = min(max(qr - WIN_R // 2, 0), rows - WIN_R)
        out.append(((rs - start) * GRID_W, start - qr + (WIN_R - 1) + ATTN_SRC_LEFT))
    return tuple(out)


def _attn_group_types(rows):
    sigs, idx = [], []
    for g in range(rows // ATTN_Q_ROWS):
        sig = _attn_row_windows(g, rows)
        if sig not in sigs:
            sigs.append(sig)
        idx.append(sigs.index(sig))
    return sigs, idx


ATTN_SRC_LEFT = ATTN_Q_ROWS - 1
ATTN_SRC_W = -(-((WIN_R - 1 + ATTN_SRC_LEFT + ATTN_K_ROWS) * GRID_W) // LANE) * LANE


def _attn_bias_source(rpb):
    nl, h, n_ro, _ = rpb.shape
    rpb = rpb.astype(F32) * (HEAD_DIM ** 0.5)
    qc = jnp.arange(GRID_W)[:, None]
    kc = jnp.arange(GRID_W)[None, :]
    co = jnp.clip(kc - qc, -(WIN_C - 1), WIN_C - 1) + (WIN_C - 1)
    pick = (jnp.arange(2 * WIN_C - 1)[:, None, None] == co[None]).astype(F32)
    toep = jnp.einsum("lhrk,kqc->lhqrc", rpb, pick, precision=lax.Precision.HIGHEST)
    cs = jnp.clip(qc - WIN_C // 2, 0, GRID_W - WIN_C)
    valid = (kc >= cs) & (kc < cs + WIN_C)
    toep = jnp.where(valid[:, None, :], toep, MASKED)
    by_q = toep.reshape(nl, h, GRID_W, n_ro * GRID_W)
    copies = []
    for shift in (0, 1):
        left = (ATTN_SRC_LEFT - shift) * GRID_W
        pad = ((0, 0), (0, 0), (0, 0), (left, ATTN_SRC_W - left - n_ro * GRID_W))
        copies.append(jnp.pad(by_q, pad, constant_values=MASKED))
    return jnp.stack(copies, axis=2)


def _attn_fill_tables(src_ref, tab_ref, rows):
    nk = ATTN_K_ROWS * GRID_W
    lane = lax.broadcasted_iota(jnp.int32, (GRID_W, nk), 1)
    for t, sig in enumerate(_attn_group_types(rows)[0]):
        for i, (lo, u) in enumerate(sig):
            copy = u % 2
            window = src_ref[copy, :, pl.ds((u - copy) * GRID_W, nk)]
            valid = (lane >= lo) & (lane < lo + WIN_R * GRID_W)
            tab_ref[t, pl.ds(i * GRID_W, GRID_W), :] = jnp.where(valid, window, MASKED)


def _attn_kernel(q_ref, k_ref, v_ref, src_ref, o_ref, tab_ref, *, rows):
    n_groups = rows // ATTN_Q_ROWS
    tq = ATTN_Q_ROWS * GRID_W
    nk = ATTN_K_ROWS * GRID_W
    exp2_scale = (HEAD_DIM ** -0.5) * 1.4426950408889634
    _attn_fill_tables(src_ref, tab_ref, rows)
    group_type = _attn_group_types(rows)[1]
    for g in range(n_groups):
        start = _attn_group_start(g, rows) * GRID_W
        q = q_ref[pl.ds(g * tq, tq), :]
        kw = k_ref[pl.ds(start, nk), :]
        vw = v_ref[pl.ds(start, nk), :]
        s = lax.dot_general(q, kw, (((1,), (1,)), ((), ())), preferred_element_type=F32)
        s = s + tab_ref[group_type[g]]
        p = jnp.exp2((s - jnp.max(s, axis=-1, keepdims=True)) * exp2_scale)
        vw_ext = jnp.concatenate([vw, jnp.ones((nk, HEAD_DIM), BF16)], axis=1)
        o = jnp.dot(p.astype(BF16), vw_ext, preferred_element_type=F32)
        o_ref[pl.ds(g * tq, tq), :] = (o[:, :HEAD_DIM] / o[:, HEAD_DIM:HEAD_DIM + 1]).astype(o_ref.dtype)


def _attention(z, bias_src, layer, batch, seq):
    rows = seq // GRID_W
    tq = ATTN_Q_ROWS * GRID_W
    tk = ATTN_K_ROWS * GRID_W
    return pl.pallas_call(
        functools.partial(_attn_kernel, rows=rows),
        grid=(batch, N_HEADS),
        in_specs=[pl.BlockSpec((seq, HEAD_DIM), lambda b, h: (b, h)),
                  pl.BlockSpec((seq, HEAD_DIM), lambda b, h: (b, N_HEADS + h)),
                  pl.BlockSpec((seq, HEAD_DIM), lambda b, h: (b, 2 * N_HEADS + h)),
                  pl.BlockSpec((None, None, 2, GRID_W, ATTN_SRC_W), lambda b, h: (layer, h, 0, 0, 0))],
        out_specs=pl.BlockSpec((seq, HEAD_DIM), lambda b, h: (b, h)),
        out_shape=jax.ShapeDtypeStruct((batch * seq, ATTN_W), BF16),
        scratch_shapes=[pltpu.VMEM((len(_attn_group_types(rows)[0]), tq, tk), F32)],
        compiler_params=_params("parallel", "parallel"),
        name="nbr_attention",
    )(z, z, z, bias_src)


def _pool_band_matrices():
    i = jnp.arange(POOL_CHUNK)[:, None]
    off = jnp.arange(POOL_CHUNK + 2 * POOL_HALO)[None, :] - POOL_HALO - i
    return jnp.stack([((off >= -(w // 2)) & (off <= w // 2 - 1)).astype(BF16) for w in POOL_WINDOWS])


def _pool_kernel(p_ref, band_ref, w_ref, sc_ref, o_ref, pad_ref, *, seq):
    g = pl.program_id(1)
    half = jnp.left_shift(1, g)
    cols = p_ref.shape[1]
    pad_ref[pl.ds(0, POOL_HALO), :] = jnp.zeros((POOL_HALO, cols), BF16)
    pad_ref[pl.ds(POOL_HALO + seq, POOL_HALO), :] = jnp.zeros((POOL_HALO, cols), BF16)
    pad_ref[pl.ds(POOL_HALO, seq), :] = p_ref[...]

    for c in range(seq // POOL_CHUNK):
        base = c * POOL_CHUNK
        ph = pad_ref[pl.ds(base, POOL_CHUNK + 2 * POOL_HALO), :]
        wsum = jnp.dot(band_ref[0], ph, preferred_element_type=F32)
        t = base + lax.broadcasted_iota(jnp.int32, (POOL_CHUNK, 1), 0)
        lo = jnp.maximum(t - half, 0)
        hi = jnp.minimum(t + half - 1, seq - 1)
        cnt = (hi - lo + 1).astype(F32)
        centre = ph[POOL_HALO:POOL_HALO + POOL_CHUNK].astype(F32)
        d = wsum / cnt - centre
        y = jnp.dot(d.astype(BF16), w_ref[0], preferred_element_type=F32) * sc_ref[...]
        o_ref[pl.ds(base, POOL_CHUNK), :] = y.astype(o_ref.dtype)


def _pool(z, bands, w_grp, scale, batch, seq, col_block0):
    n_g = len(POOL_WINDOWS)
    return pl.pallas_call(
        functools.partial(_pool_kernel, seq=seq),
        grid=(batch, n_g),
        in_specs=[pl.BlockSpec((seq, POOL_GROUP), lambda b, g: (b, col_block0 + g)),
                  pl.BlockSpec((1, POOL_CHUNK, POOL_CHUNK + 2 * POOL_HALO), lambda b, g: (g, 0, 0)),
                  pl.BlockSpec((1, POOL_GROUP, POOL_GROUP), lambda b, g: (g, 0, 0)),
                  pl.BlockSpec((1, POOL_GROUP), lambda b, g: (0, g))],
        out_specs=pl.BlockSpec((seq, POOL_GROUP), lambda b, g: (b, g)),
        out_shape=jax.ShapeDtypeStruct((batch * seq, POOL_W), BF16),
        scratch_shapes=[pltpu.VMEM((seq + 2 * POOL_HALO, POOL_GROUP), BF16)],
        compiler_params=_params("parallel", "arbitrary"),
        name="multiscale_pool",
    )(z, bands, w_grp, scale.reshape(1, POOL_W))


def _gelu_tanh(x):
    return 0.5 * x * (1.0 + jnp.tanh(0.7978845608028654 * (x + 0.044715 * (x * x * x))))


def _gmlp_kernel(u_ref, v_ref, lng_ref, lnb_ref, ws_ref, bs_ref, o_ref):
    v = _gelu_tanh(v_ref[...].astype(F32))
    mu = jnp.mean(v, axis=-1, keepdims=True)
    vc = v - mu
    var = jnp.mean(vc * vc, axis=-1, keepdims=True)
    vn = (vc * lax.rsqrt(var + EPS) * lng_ref[...] + lnb_ref[...]).astype(BF16)
    n_chunks = u_ref.shape[0] // GMLP_CHUNK
    for c in range(n_chunks):
        r = slice(c * GMLP_CHUNK, (c + 1) * GMLP_CHUNK)
        for g in range(GMLP_GROUPS):
            cs = slice(g * GMLP_GROUP_W, (g + 1) * GMLP_GROUP_W)
            mixed = jnp.dot(ws_ref[g], vn[r, cs], preferred_element_type=F32) + bs_ref[g]
            u = _gelu_tanh(u_ref[r, cs].astype(F32))
            o_ref[r, cs] = (u * mixed).astype(o_ref.dtype)


def _gmlp(z, ln_g, ln_b, w_s, b_s, u_block, tm=512):
    m = z.shape[0]
    return pl.pallas_call(
        _gmlp_kernel,
        grid=(m // tm,),
        in_specs=[pl.BlockSpec((tm, GMLP_W), lambda i: (i, u_block)),
                  pl.BlockSpec((tm, GMLP_W), lambda i: (i, u_block + 1)),
                  pl.BlockSpec((1, GMLP_W), lambda i: (0, 0)),
                  pl.BlockSpec((1, GMLP_W), lambda i: (0, 0)),
                  pl.BlockSpec((GMLP_GROUPS, GMLP_CHUNK, GMLP_CHUNK), lambda i: (0, 0, 0)),
                  pl.BlockSpec((GMLP_GROUPS, GMLP_CHUNK, 1), lambda i: (0, 0, 0))],
        out_specs=pl.BlockSpec((tm, GMLP_W), lambda i: (i, 0)),
        out_shape=jax.ShapeDtypeStruct((m, GMLP_W), BF16),
        compiler_params=_params("parallel"),
        name="spatial_gating",
    )(z, z, ln_g.reshape(1, GMLP_W), ln_b.reshape(1, GMLP_W), w_s, b_s.reshape(GMLP_GROUPS, GMLP_CHUNK, 1))


def _sigmoid(x):
    return 1.0 / (1.0 + jnp.exp(-x))


def _merge_kernel(ya_ref, yp_ref, ys_ref, hd_ref, wb_ref, gu0_ref, gu1_ref, gu2_ref,
                  gb0_ref, gb1_ref, gb2_ref, o_ref, wb0_ref, gus0_ref, wb1_ref, gus1_ref):
    def compute(slots):
        wbb_ref, gub_ref = slots[0], slots[1]
        hd = hd_ref[...]
        o_pool = ATTN_W
        o_sg = ATTN_W + POOL_W
        ga = _sigmoid(jnp.dot(hd, gub_ref[0], preferred_element_type=F32) + gb0_ref[...])
        acc = ga * jnp.dot(ya_ref[...], wbb_ref[pl.ds(0, ATTN_W), :], preferred_element_type=F32)
        gp = _sigmoid(jnp.dot(hd, gub_ref[1], preferred_element_type=F32) + gb1_ref[...])
        acc = acc + gp * jnp.dot(yp_ref[...], wbb_ref[pl.ds(o_pool, POOL_W), :], preferred_element_type=F32)
        gs = _sigmoid(jnp.dot(hd, gub_ref[2], preferred_element_type=F32) + gb2_ref[...])
        acc = acc + gs * jnp.dot(ys_ref[...], wbb_ref[pl.ds(o_sg, GMLP_W), :], preferred_element_type=F32)
        o_ref[...] = acc.astype(o_ref.dtype)

    def fill_targets(wbb_ref, gub_ref):
        return [wbb_ref, gub_ref.at[0], gub_ref.at[1], gub_ref.at[2]]

    w_refs = [wb_ref, gu0_ref, gu1_ref, gu2_ref]
    j = pl.program_id(0)

    @pl.when(j == 0)
    def _():
        _pw_fill(w_refs, fill_targets(wb0_ref, gus0_ref))

    @pl.when((j > 0) & (j % 2 == 1))
    def _():
        _pw_fill(w_refs, fill_targets(wb1_ref, gus1_ref))
        compute([wb0_ref, gus0_ref])

    @pl.when((j > 0) & (j % 2 == 0))
    def _():
        _pw_fill(w_refs, fill_targets(wb0_ref, gus0_ref))
        compute([wb1_ref, gus1_ref])


def _merge(ya, yp, ys, hd, wb_stack, gate_up_stack, gate_b, layer, *, tm, tn):
    m = ya.shape[0]
    mix_w, d = wb_stack.shape[1:]
    rank = hd.shape[1]
    n_n, n_m = d // tn, m // tm
    assert mix_w % n_m == 0 and rank % n_m == 0 and (rank // n_m) % 16 == 0, (mix_w, rank, n_m)
    m_idx = lambda j, i: jnp.where(j == 0, 0, i)
    n_idx = lambda j: jnp.maximum(j - 1, 0)
    chunk = lambda j, i: jnp.where(j < n_n, i, n_m - 1)
    tile = lambda j: jnp.minimum(j, n_n - 1)
    row = lambda w: pl.BlockSpec((tm, w), lambda j, i: (m_idx(j, i), 0))
    gate_b = gate_b.reshape(1, N_BRANCH * d)
    gu = lambda br: pl.BlockSpec((None, rank // n_m, tn),
                                 lambda j, i: (layer, chunk(j, i), br * n_n + tile(j)))
    gb = lambda br: pl.BlockSpec((1, tn), lambda j, i: (0, br * n_n + n_idx(j)))
    slot_shapes = [pltpu.VMEM((mix_w, tn), BF16), pltpu.VMEM((N_BRANCH, rank, tn), BF16)]
    return pl.pallas_call(
        _merge_kernel,
        grid=(n_n + 1, n_m),
        in_specs=[row(ATTN_W), row(POOL_W), row(GMLP_W), row(rank),
                  pl.BlockSpec((None, mix_w // n_m, tn), lambda j, i: (layer, chunk(j, i), tile(j))),
                  gu(0), gu(1), gu(2), gb(0), gb(1), gb(2)],
        out_specs=pl.BlockSpec((tm, tn), lambda j, i: (m_idx(j, i), n_idx(j))),
        out_shape=jax.ShapeDtypeStruct((m, d), BF16),
        scratch_shapes=slot_shapes + slot_shapes,
        compiler_params=_params("arbitrary", "arbitrary"),
        name="branch_merge",
    )(ya, yp, ys, hd, wb_stack, gate_up_stack, gate_up_stack, gate_up_stack, gate_b, gate_b, gate_b)


def kernel(x, attn_norm_g, w_in, rpb, pool_w, pool_scale, gmlp_ln_g, gmlp_ln_b, gmlp_w_s, gmlp_b_s,
           w_branch, gate_down, gate_up, gate_b, w_out, ffn_norm_g, w_ffn_gate, w_ffn_up, w_ffn_down,
           final_norm_g):
    batch, seq, d = x.shape
    depth = w_in.shape[0]
    xf = x.reshape(batch * seq, d)
    bands = _pool_band_matrices()
    pool_col_block = (3 * ATTN_W) // POOL_GROUP
    u_block = (3 * ATTN_W + POOL_W) // GMLP_W
    bias_src = _attn_bias_source(rpb)
    hu, rs = _prenorm(xf, attn_norm_g[0])
    for l in range(depth):
        z = _pw_call(_pw_mm_kernel, hu, [w_in], l, tm=1024, tn=1024, out_dtype=BF16, row_scale=rs,
                     name="in_proj")
        hd = _matmul_ws(hu, gate_down, l, rs, tm=1024, tn=gate_down.shape[2], out_dtype=BF16,
                        name="gate_down")
        y_attn = _attention(z, bias_src, l, batch, seq)
        y_pool = _pool(z, bands, pool_w[l].astype(BF16), pool_scale[l], batch, seq, pool_col_block)
        y_sg = _gmlp(z, gmlp_ln_g[l], gmlp_ln_b[l], gmlp_w_s[l].astype(BF16), gmlp_b_s[l], u_block)
        merged = _merge(y_attn, y_pool, y_sg, hd, w_branch, gate_up, gate_b[l], l, tm=1024, tn=1024)
        xf, hu, ssq = _pw_call(_pw_mm_res_norm_kernel, merged, [w_out], l, tm=1024, tn=512, out_dtype=F32,
                               residual=xf, next_gain=ffn_norm_g[l], name="out_proj")
        rs = _row_scale(ssq, d)
        act = _pw_call(_pw_swiglu_kernel, hu, [w_ffn_gate, w_ffn_up], l, tm=2048, tn=256, out_dtype=BF16,
                       row_scale=rs, name="swiglu")
        if l + 1 < depth:
            xf, hu, ssq = _pw_call(_pw_mm_res_norm_kernel, act, [w_ffn_down], l, tm=512, tn=512,
                                   out_dtype=F32, residual=xf, next_gain=attn_norm_g[l + 1],
                                   name="ffn_down")
            rs = _row_scale(ssq, d)
        else:
            xf = _pw_call(_pw_mm_res_kernel, act, [w_ffn_down], l, tm=512, tn=512, out_dtype=F32,
                          residual=xf, name="ffn_down")
    out = _rmsnorm(xf, final_norm_g, F32)
    return out.reshape(batch, seq, d)
```

```python
import functools

import jax
import jax.numpy as jnp
from jax import lax
from jax.experimental import pallas as pl
from jax.experimental.pallas import tpu as pltpu

F32 = jnp.float32
BF16 = jnp.bfloat16

GRID_W = 64
N_HEADS = 16
HEAD_DIM = 128
ATTN_W = N_HEADS * HEAD_DIM
WIN_R = 8
WIN_C = 16
POOL_WINDOWS = (2, 4, 8, 16)
POOL_GROUP = 256
POOL_W = POOL_GROUP * len(POOL_WINDOWS)
GMLP_CHUNK = 128
GMLP_GROUPS = 4
GMLP_GROUP_W = 256
GMLP_W = GMLP_GROUPS * GMLP_GROUP_W
N_BRANCH = 3
EPS = 1e-6

VMEM_LIMIT_BYTES = 62 * 1024 * 1024
LANE = 128

ATTN_Q_ROWS = 4
ATTN_K_ROWS = ATTN_Q_ROWS + WIN_R - 1
SWIGLU_WIDE_TN = 768
SWIGLU_TAIL_TN = 256
POOL_CHUNK = 256
POOL_HALO = 128
MASKED = -1e30


def _params(*sem):
    return pltpu.CompilerParams(dimension_semantics=sem, vmem_limit_bytes=VMEM_LIMIT_BYTES)


def _rmsnorm_kernel(x_ref, g_ref, o_ref):
    x = x_ref[...]
    ms = jnp.mean(x * x, axis=-1, keepdims=True)
    o_ref[...] = (x * lax.rsqrt(ms + EPS) * g_ref[...]).astype(o_ref.dtype)


def _rmsnorm(x, g, out_dtype, tm=256):
    m, d = x.shape
    return pl.pallas_call(
        _rmsnorm_kernel,
        grid=(m // tm,),
        in_specs=[pl.BlockSpec((tm, d), lambda i: (i, 0)),
                  pl.BlockSpec((1, d), lambda i: (0, 0))],
        out_specs=pl.BlockSpec((tm, d), lambda i: (i, 0)),
        out_shape=jax.ShapeDtypeStruct((m, d), out_dtype),
        compiler_params=_params("parallel"),
        name="rmsnorm",
    )(x, g.reshape(1, d))


def _prenorm_kernel(x_ref, g_ref, hu_ref, rs_ref):
    x = x_ref[...]
    ms = jnp.mean(x * x, axis=-1, keepdims=True)
    hu_ref[...] = (x * g_ref[...]).astype(hu_ref.dtype)
    rs_ref[...] = jnp.broadcast_to(lax.rsqrt(ms + EPS), rs_ref.shape)


def _prenorm(x, g, tm=256):
    m, d = x.shape
    return pl.pallas_call(
        _prenorm_kernel,
        grid=(m // tm,),
        in_specs=[pl.BlockSpec((tm, d), lambda i: (i, 0)),
                  pl.BlockSpec((1, d), lambda i: (0, 0))],
        out_specs=[pl.BlockSpec((tm, d), lambda i: (i, 0)),
                   pl.BlockSpec((tm, LANE), lambda i: (i, 0))],
        out_shape=[jax.ShapeDtypeStruct((m, d), BF16), jax.ShapeDtypeStruct((m, LANE), F32)],
        compiler_params=_params("parallel"),
        name="prenorm",
    )(x, g.reshape(1, d))


def _row_scale_kernel(ssq_ref, rs_ref, *, d):
    rs_ref[...] = lax.rsqrt(jnp.sum(ssq_ref[...], axis=0) * (1.0 / d) + EPS)


def _row_scale(ssq_parts, d, tm=1024):
    n_t, m, _ = ssq_parts.shape
    return pl.pallas_call(
        functools.partial(_row_scale_kernel, d=d),
        grid=(m // tm,),
        in_specs=[pl.BlockSpec((n_t, tm, LANE), lambda i: (0, i, 0))],
        out_specs=pl.BlockSpec((tm, LANE), lambda i: (i, 0)),
        out_shape=jax.ShapeDtypeStruct((m, LANE), F32),
        compiler_params=_params("parallel"),
        name="row_scale",
    )(ssq_parts)


def _cast_on_first_m_step(pairs):
    @pl.when(pl.program_id(1) == 0)
    def _():
        for src, dst in pairs:
            dst[...] = src[...].astype(BF16)


def _ws_mm_kernel(a_ref, w_ref, rs_ref, o_ref, wb_ref):
    _cast_on_first_m_step([(w_ref, wb_ref)])
    acc = jnp.dot(a_ref[...], wb_ref[...], preferred_element_type=F32)
    o_ref[...] = (acc * rs_ref[:, :1]).astype(o_ref.dtype)


def _matmul_ws(a, w_stack, layer, row_scale, *, tm, tn, out_dtype, name):
    m, k = a.shape
    n = w_stack.shape[2]
    return pl.pallas_call(
        _ws_mm_kernel,
        grid=(n // tn, m // tm),
        in_specs=[pl.BlockSpec((tm, k), lambda j, i: (i, 0)),
                  pl.BlockSpec((None, k, tn), lambda j, i: (layer, 0, j)),
                  pl.BlockSpec((tm, LANE), lambda j, i: (i, 0))],
        out_specs=pl.BlockSpec((tm, tn), lambda j, i: (i, j)),
        out_shape=jax.ShapeDtypeStruct((m, n), out_dtype),
        scratch_shapes=[pltpu.VMEM((k, tn), BF16)],
        compiler_params=_params("arbitrary", "arbitrary"),
        name=name,
    )(a, w_stack, row_scale)


def _pw_fill(w_refs, slot_refs):
    for w_ref, slot_ref in zip(w_refs, slot_refs):
        rows = w_ref.shape[0]
        r0 = pl.multiple_of(pl.program_id(1) * rows, rows)
        slot_ref[pl.ds(r0, rows), :] = w_ref[...].astype(BF16)


def _pw_phases(w_refs, even_slots, odd_slots, compute):
    j = pl.program_id(0)

    @pl.when(j == 0)
    def _():
        _pw_fill(w_refs, even_slots)

    @pl.when((j > 0) & (j % 2 == 1))
    def _():
        _pw_fill(w_refs, odd_slots)
        compute(even_slots)

    @pl.when((j > 0) & (j % 2 == 0))
    def _():
        _pw_fill(w_refs, even_slots)
        compute(odd_slots)


def _pw_mm_kernel(a_ref, w_ref, rs_ref, o_ref, w0_ref, w1_ref):
    def compute(slots):
        acc = jnp.dot(a_ref[...], slots[0][...], preferred_element_type=F32)
        o_ref[...] = (acc * rs_ref[:, :1]).astype(o_ref.dtype)
    _pw_phases([w_ref], [w0_ref], [w1_ref], compute)


def _pw_mm_res_kernel(a_ref, w_ref, r_ref, o_ref, w0_ref, w1_ref):
    def compute(slots):
        o_ref[...] = r_ref[...] + jnp.dot(a_ref[...], slots[0][...], preferred_element_type=F32)
    _pw_phases([w_ref], [w0_ref], [w1_ref], compute)


def _pw_mm_res_norm_kernel(a_ref, w_ref, r_ref, g_ref, o_ref, hu_ref, ssq_ref, w0_ref, w1_ref):
    def compute(slots):
        x = r_ref[...] + jnp.dot(a_ref[...], slots[0][...], preferred_element_type=F32)
        o_ref[...] = x
        hu_ref[...] = (x * g_ref[...]).astype(hu_ref.dtype)
        ssq_ref[...] = jnp.broadcast_to(jnp.sum(x * x, axis=-1, keepdims=True), ssq_ref.shape)
    _pw_phases([w_ref], [w0_ref], [w1_ref], compute)


def _pw_swiglu_kernel(h_ref, wg_ref, wu_ref, rs_ref, o_ref, g0_ref, u0_ref, g1_ref, u1_ref):
    def compute(slots):
        h = h_ref[...]
        rs = rs_ref[:, :1]
        a = jnp.dot(h, slots[0][...], preferred_element_type=F32) * rs
        b = jnp.dot(h, slots[1][...], preferred_element_type=F32) * rs
        o_ref[...] = (a * (1.0 / (1.0 + jnp.exp(-a))) * b).astype(o_ref.dtype)
    _pw_phases([wg_ref, wu_ref], [g0_ref, u0_ref], [g1_ref, u1_ref], compute)


def _pw_swiglu_into_kernel(h_ref, wg_ref, wu_ref, rs_ref, into_ref, o_ref, g0_ref, u0_ref, g1_ref, u1_ref):
    del into_ref
    _pw_swiglu_kernel(h_ref, wg_ref, wu_ref, rs_ref, o_ref, g0_ref, u0_ref, g1_ref, u1_ref)


def _pw_call(kern, a, w_stacks, layer, *, tm, tn, out_dtype, name, row_scale=None, residual=None,
             next_gain=None, tiles=None, into=None):
    m, k = a.shape
    n = w_stacks[0].shape[2]
    first, n_n = (0, n // tn) if tiles is None else tiles
    assert (first + n_n) * tn <= n and (tiles is not None or n % tn == 0), (n, tn, tiles)
    n_m = m // tm
    rows = k // n_m
    assert k % n_m == 0 and rows % 16 == 0, (k, n_m)
    m_idx = lambda j, i: jnp.where(j == 0, 0, i)
    n_idx = lambda j: first + jnp.maximum(j - 1, 0)
    w_spec = pl.BlockSpec((None, rows, tn), lambda j, i: (layer, jnp.where(j < n_n, i, n_m - 1),
                                                          first + jnp.minimum(j, n_n - 1)))
    o_spec = pl.BlockSpec((tm, tn), lambda j, i: (m_idx(j, i), n_idx(j)))
    in_specs = [pl.BlockSpec((tm, k), lambda j, i: (m_idx(j, i), 0))] + [w_spec] * len(w_stacks)
    args = [a, *w_stacks]
    out_specs = o_spec
    out_shape = jax.ShapeDtypeStruct((m, n), out_dtype)
    aliases = {}
    if row_scale is not None:
        in_specs.append(pl.BlockSpec((tm, LANE), lambda j, i: (m_idx(j, i), 0)))
        args.append(row_scale)
    if residual is not None:
        in_specs.append(o_spec)
        args.append(residual)
    if next_gain is not None:
        in_specs.append(pl.BlockSpec((1, tn), lambda j, i: (0, n_idx(j))))
        args.append(next_gain.reshape(1, n))
        out_specs = [o_spec, o_spec, pl.BlockSpec((None, tm, LANE), lambda j, i: (n_idx(j), m_idx(j, i), 0))]
        out_shape = [out_shape, jax.ShapeDtypeStruct((m, n), BF16),
                     jax.ShapeDtypeStruct((n_n, m, LANE), F32)]
    if into is not None:
        assert next_gain is None and into.shape == (m, n) and into.dtype == out_dtype
        aliases = {len(args): 0}
        in_specs.append(pl.BlockSpec(memory_space=pl.ANY))
        args.append(into)
    return pl.pallas_call(
        kern,
        grid=(n_n + 1, n_m),
        in_specs=in_specs,
        out_specs=out_specs,
        out_shape=out_shape,
        scratch_shapes=[pltpu.VMEM((k, tn), BF16)] * (2 * len(w_stacks)),
        input_output_aliases=aliases,
        compiler_params=_params("arbitrary", "arbitrary"),
        name=name,
    )(*args)


def _attn_group_start(g, rows):
    return min(max(g * ATTN_Q_ROWS - WIN_R // 2, 0), rows - ATTN_K_ROWS)


def _attn_row_windows(g, rows):
    start = _attn_group_start(g, rows)
    out = []
    for i in range(ATTN_Q_ROWS):
        qr = g * ATTN_Q_ROWS + i
        rs = min(max(qr - WIN_R // 2, 0), rows - WIN_R)
        out.append(((rs - start) * GRID_W, start - qr + (WIN_R - 1) + ATTN_SRC_LEFT))
    return tuple(out)


def _attn_group_types(rows):
    sigs, idx = [], []
    for g in range(rows // ATTN_Q_ROWS):
        sig = _attn_row_windows(g, rows)
        if sig not in sigs:
            sigs.append(sig)
        idx.append(sigs.index(sig))
    return sigs, idx


ATTN_SRC_LEFT = ATTN_Q_ROWS - 1
ATTN_SRC_W = -(-((WIN_R - 1 + ATTN_SRC_LEFT + ATTN_K_ROWS) * GRID_W) // LANE) * LANE


def _attn_bias_source(rpb):
    nl, h, n_ro, _ = rpb.shape
    rpb = rpb.astype(F32) * (HEAD_DIM ** 0.5)
    qc = jnp.arange(GRID_W)[:, None]
    kc = jnp.arange(GRID_W)[None, :]
    co = jnp.clip(kc - qc, -(WIN_C - 1), WIN_C - 1) + (WIN_C - 1)
    pick = (jnp.arange(2 * WIN_C - 1)[:, None, None] == co[None]).astype(F32)
    toep = jnp.einsum("lhrk,kqc->lhqrc", rpb, pick, precision=lax.Precision.HIGHEST)
    cs = jnp.clip(qc - WIN_C // 2, 0, GRID_W - WIN_C)
    valid = (kc >= cs) & (kc < cs + WIN_C)
    toep = jnp.where(valid[:, None, :], toep, MASKED)
    by_q = toep.reshape(nl, h, GRID_W, n_ro * GRID_W)
    copies = []
    for shift in (0, 1):
        left = (ATTN_SRC_LEFT - shift) * GRID_W
        pad = ((0, 0), (0, 0), (0, 0), (left, ATTN_SRC_W - left - n_ro * GRID_W))
        copies.append(jnp.pad(by_q, pad, constant_values=MASKED))
    return jnp.stack(copies, axis=2)


def _attn_fill_tables(src_ref, tab_ref, rows):
    nk = ATTN_K_ROWS * GRID_W
    lane = lax.broadcasted_iota(jnp.int32, (GRID_W, nk), 1)
    for t, sig in enumerate(_attn_group_types(rows)[0]):
        for i, (lo, u) in enumerate(sig):
            copy = u % 2
            window = src_ref[copy, :, pl.ds((u - copy) * GRID_W, nk)]
            valid = (lane >= lo) & (lane < lo + WIN_R * GRID_W)
            tab_ref[t, pl.ds(i * GRID_W, GRID_W), :] = jnp.where(valid, window, MASKED)


def _attn_kernel(q_ref, k_ref, v_ref, src_ref, o_ref, tab_ref, *, rows):
    n_groups = rows // ATTN_Q_ROWS
    tq = ATTN_Q_ROWS * GRID_W
    nk = ATTN_K_ROWS * GRID_W
    exp2_scale = (HEAD_DIM ** -0.5) * 1.4426950408889634
    _attn_fill_tables(src_ref, tab_ref, rows)
    group_type = _attn_group_types(rows)[1]
    for g in range(n_groups):
        start = _attn_group_start(g, rows) * GRID_W
        q = q_ref[pl.ds(g * tq, tq), :]
        kw = k_ref[pl.ds(start, nk), :]
        vw = v_ref[pl.ds(start, nk), :]
        s = lax.dot_general(q, kw, (((1,), (1,)), ((), ())), preferred_element_type=F32)
        s = s + tab_ref[group_type[g]]
        p = jnp.exp2((s - jnp.max(s, axis=-1, keepdims=True)) * exp2_scale)
        vw_ext = jnp.concatenate([vw, jnp.ones((nk, HEAD_DIM), BF16)], axis=1)
        o = jnp.dot(p.astype(BF16), vw_ext, preferred_element_type=F32)
        o_ref[pl.ds(g * tq, tq), :] = (o[:, :HEAD_DIM] / o[:, HEAD_DIM:HEAD_DIM + 1]).astype(o_ref.dtype)


def _attention(z, bias_src, layer, batch, seq):
    rows = seq // GRID_W
    tq = ATTN_Q_ROWS * GRID_W
    tk = ATTN_K_ROWS * GRID_W
    return pl.pallas_call(
        functools.partial(_attn_kernel, rows=rows),
        grid=(batch, N_HEADS),
        in_specs=[pl.BlockSpec((seq, HEAD_DIM), lambda b, h: (b, h)),
                  pl.BlockSpec((seq, HEAD_DIM), lambda b, h: (b, N_HEADS + h)),
                  pl.BlockSpec((seq, HEAD_DIM), lambda b, h: (b, 2 * N_HEADS + h)),
                  pl.BlockSpec((None, None, 2, GRID_W, ATTN_SRC_W), lambda b, h: (layer, h, 0, 0, 0))],
        out_specs=pl.BlockSpec((seq, HEAD_DIM), lambda b, h: (b, h)),
        out_shape=jax.ShapeDtypeStruct((batch * seq, ATTN_W), BF16),
        scratch_shapes=[pltpu.VMEM((len(_attn_group_types(rows)[0]), tq, tk), F32)],
        compiler_params=_params("parallel", "parallel"),
        name="nbr_attention",
    )(z, z, z, bias_src)


def _pool_band_matrices():
    i = jnp.arange(POOL_CHUNK)[:, None]
    off = jnp.arange(POOL_CHUNK + 2 * POOL_HALO)[None, :] - POOL_HALO - i
    return jnp.stack([((off >= -(w // 2)) & (off <= w // 2 - 1)).astype(BF16) for w in POOL_WINDOWS])


def _pool_kernel(p_ref, band_ref, w_ref, sc_ref, o_ref, pad_ref, *, seq):
    g = pl.program_id(1)
    half = jnp.left_shift(1, g)
    cols = p_ref.shape[1]
    pad_ref[pl.ds(0, POOL_HALO), :] = jnp.zeros((POOL_HALO, cols), BF16)
    pad_ref[pl.ds(POOL_HALO + seq, POOL_HALO), :] = jnp.zeros((POOL_HALO, cols), BF16)
    pad_ref[pl.ds(POOL_HALO, seq), :] = p_ref[...]

    for c in range(seq // POOL_CHUNK):
        base = c * POOL_CHUNK
        ph = pad_ref[pl.ds(base, POOL_CHUNK + 2 * POOL_HALO), :]
        wsum = jnp.dot(band_ref[0], ph, preferred_element_type=F32)
        t = base + lax.broadcasted_iota(jnp.int32, (POOL_CHUNK, 1), 0)
        lo = jnp.maximum(t - half, 0)
        hi = jnp.minimum(t + half - 1, seq - 1)
        cnt = (hi - lo + 1).astype(F32)
        centre = ph[POOL_HALO:POOL_HALO + POOL_CHUNK].astype(F32)
        d = wsum / cnt - centre
        y = jnp.dot(d.astype(BF16), w_ref[0], preferred_element_type=F32) * sc_ref[...]
        o_ref[pl.ds(base, POOL_CHUNK), :] = y.astype(o_ref.dtype)


def _pool(z, bands, w_grp, scale, batch, seq, col_block0):
    n_g = len(POOL_WINDOWS)
    return pl.pallas_call(
        functools.partial(_pool_kernel, seq=seq),
        grid=(batch, n_g),
        in_specs=[pl.BlockSpec((seq, POOL_GROUP), lambda b, g: (b, col_block0 + g)),
                  pl.BlockSpec((1, POOL_CHUNK, POOL_CHUNK + 2 * POOL_HALO), lambda b, g: (g, 0, 0)),
                  pl.BlockSpec((1, POOL_GROUP, POOL_GROUP), lambda b, g: (g, 0, 0)),
                  pl.BlockSpec((1, POOL_GROUP), lambda b, g: (0, g))],
        out_specs=pl.BlockSpec((seq, POOL_GROUP), lambda b, g: (b, g)),
        out_shape=jax.ShapeDtypeStruct((batch * seq, POOL_W), BF16),
        scratch_shapes=[pltpu.VMEM((seq + 2 * POOL_HALO, POOL_GROUP), BF16)],
        compiler_params=_params("parallel", "arbitrary"),
        name="multiscale_pool",
    )(z, bands, w_grp, scale.reshape(1, POOL_W))


def _gelu_tanh(x):
    return 0.5 * x * (1.0 + jnp.tanh(0.7978845608028654 * (x + 0.044715 * (x * x * x))))


def _gmlp_kernel(u_ref, v_ref, lng_ref, lnb_ref, ws_ref, bs_ref, o_ref):
    v = _gelu_tanh(v_ref[...].astype(F32))
    mu = jnp.mean(v, axis=-1, keepdims=True)
    vc = v - mu
    var = jnp.mean(vc * vc, axis=-1, keepdims=True)
    vn = (vc * lax.rsqrt(var + EPS) * lng_ref[...] + lnb_ref[...]).astype(BF16)
    n_chunks = u_ref.shape[0] // GMLP_CHUNK
    for c in range(n_chunks):
        r = slice(c * GMLP_CHUNK, (c + 1) * GMLP_CHUNK)
        for g in range(GMLP_GROUPS):
            cs = slice(g * GMLP_GROUP_W, (g + 1) * GMLP_GROUP_W)
            mixed = jnp.dot(ws_ref[g], vn[r, cs], preferred_element_type=F32) + bs_ref[g]
            u = _gelu_tanh(u_ref[r, cs].astype(F32))
            o_ref[r, cs] = (u * mixed).astype(o_ref.dtype)


def _gmlp(z, ln_g, ln_b, w_s, b_s, u_block, tm=512):
    m = z.shape[0]
    return pl.pallas_call(
        _gmlp_kernel,
        grid=(m // tm,),
        in_specs=[pl.BlockSpec((tm, GMLP_W), lambda i: (i, u_block)),
                  pl.BlockSpec((tm, GMLP_W), lambda i: (i, u_block + 1)),
                  pl.BlockSpec((1, GMLP_W), lambda i: (0, 0)),
                  pl.BlockSpec((1, GMLP_W), lambda i: (0, 0)),
                  pl.BlockSpec((GMLP_GROUPS, GMLP_CHUNK, GMLP_CHUNK), lambda i: (0, 0, 0)),
                  pl.BlockSpec((GMLP_GROUPS, GMLP_CHUNK, 1), lambda i: (0, 0, 0))],
        out_specs=pl.BlockSpec((tm, GMLP_W), lambda i: (i, 0)),
        out_shape=jax.ShapeDtypeStruct((m, GMLP_W), BF16),
        compiler_params=_params("parallel"),
        name="spatial_gating",
    )(z, z, ln_g.reshape(1, GMLP_W), ln_b.reshape(1, GMLP_W), w_s, b_s.reshape(GMLP_GROUPS, GMLP_CHUNK, 1))


def _sigmoid(x):
    return 1.0 / (1.0 + jnp.exp(-x))


def _merge_kernel(ya_ref, yp_ref, ys_ref, hd_ref, wb_ref, gu0_ref, gu1_ref, gu2_ref,
                  gb0_ref, gb1_ref, gb2_ref, o_ref, wb0_ref, gus0_ref, wb1_ref, gus1_ref):
    def compute(slots):
        wbb_ref, gub_ref = slots[0], slots[1]
        hd = hd_ref[...]
        o_pool = ATTN_W
        o_sg = ATTN_W + POOL_W
        ga = _sigmoid(jnp.dot(hd, gub_ref[0], preferred_element_type=F32) + gb0_ref[...])
        acc = ga * jnp.dot(ya_ref[...], wbb_ref[pl.ds(0, ATTN_W), :], preferred_element_type=F32)
        gp = _sigmoid(jnp.dot(hd, gub_ref[1], preferred_element_type=F32) + gb1_ref[...])
        acc = acc + gp * jnp.dot(yp_ref[...], wbb_ref[pl.ds(o_pool, POOL_W), :], preferred_element_type=F32)
        gs = _sigmoid(jnp.dot(hd, gub_ref[2], preferred_element_type=F32) + gb2_ref[...])
        acc = acc + gs * jnp.dot(ys_ref[...], wbb_ref[pl.ds(o_sg, GMLP_W), :], preferred_element_type=F32)
        o_ref[...] = acc.astype(o_ref.dtype)

    def fill_targets(wbb_ref, gub_ref):
        return [wbb_ref, gub_ref.at[0], gub_ref.at[1], gub_ref.at[2]]

    w_refs = [wb_ref, gu0_ref, gu1_ref, gu2_ref]
    j = pl.program_id(0)

    @pl.when(j == 0)
    def _():
        _pw_fill(w_refs, fill_targets(wb0_ref, gus0_ref))

    @pl.when((j > 0) & (j % 2 == 1))
    def _():
        _pw_fill(w_refs, fill_targets(wb1_ref, gus1_ref))
        compute([wb0_ref, gus0_ref])

    @pl.when((j > 0) & (j % 2 == 0))
    def _():
        _pw_fill(w_refs, fill_targets(wb0_ref, gus0_ref))
        compute([wb1_ref, gus1_ref])


def _merge(ya, yp, ys, hd, wb_stack, gate_up_stack, gate_b, layer, *, tm, tn):
    m = ya.shape[0]
    mix_w, d = wb_stack.shape[1:]
    rank = hd.shape[1]
    n_n, n_m = d // tn, m // tm
    assert mix_w % n_m == 0 and rank % n_m == 0 and (rank // n_m) % 16 == 0, (mix_w, rank, n_m)
    m_idx = lambda j, i: jnp.where(j == 0, 0, i)
    n_idx = lambda j: jnp.maximum(j - 1, 0)
    chunk = lambda j, i: jnp.where(j < n_n, i, n_m - 1)
    tile = lambda j: jnp.minimum(j, n_n - 1)
    row = lambda w: pl.BlockSpec((tm, w), lambda j, i: (m_idx(j, i), 0))
    gate_b = gate_b.reshape(1, N_BRANCH * d)
    gu = lambda br: pl.BlockSpec((None, rank // n_m, tn),
                                 lambda j, i: (layer, chunk(j, i), br * n_n + tile(j)))
    gb = lambda br: pl.BlockSpec((1, tn), lambda j, i: (0, br * n_n + n_idx(j)))
    slot_shapes = [pltpu.VMEM((mix_w, tn), BF16), pltpu.VMEM((N_BRANCH, rank, tn), BF16)]
    return pl.pallas_call(
        _merge_kernel,
        grid=(n_n + 1, n_m),
        in_specs=[row(ATTN_W), row(POOL_W), row(GMLP_W), row(rank),
                  pl.BlockSpec((None, mix_w // n_m, tn), lambda j, i: (layer, chunk(j, i), tile(j))),
                  gu(0), gu(1), gu(2), gb(0), gb(1), gb(2)],
        out_specs=pl.BlockSpec((tm, tn), lambda j, i: (m_idx(j, i), n_idx(j))),
        out_shape=jax.ShapeDtypeStruct((m, d), BF16),
        scratch_shapes=slot_shapes + slot_shapes,
        compiler_params=_params("arbitrary", "arbitrary"),
        name="branch_merge",
    )(ya, yp, ys, hd, wb_stack, gate_up_stack, gate_up_stack, gate_up_stack, gate_b, gate_b, gate_b)


def kernel(x, attn_norm_g, w_in, rpb, pool_w, pool_scale, gmlp_ln_g, gmlp_ln_b, gmlp_w_s, gmlp_b_s,
           w_branch, gate_down, gate_up, gate_b, w_out, ffn_norm_g, w_ffn_gate, w_ffn_up, w_ffn_down,
           final_norm_g):
    batch, seq, d = x.shape
    depth = w_in.shape[0]
    d_ff = w_ffn_gate.shape[2]
    xf = x.reshape(batch * seq, d)
    bands = _pool_band_matrices()
    pool_col_block = (3 * ATTN_W) // POOL_GROUP
    u_block = (3 * ATTN_W + POOL_W) // GMLP_W
    bias_src = _attn_bias_source(rpb)
    hu, rs = _prenorm(xf, attn_norm_g[0])
    for l in range(depth):
        z = _pw_call(_pw_mm_kernel, hu, [w_in], l, tm=1024, tn=1024, out_dtype=BF16, row_scale=rs,
                     name="in_proj")
        hd = _matmul_ws(hu, gate_down, l, rs, tm=1024, tn=gate_down.shape[2], out_dtype=BF16,
                        name="gate_down")
        y_attn = _attention(z, bias_src, l, batch, seq)
        y_pool = _pool(z, bands, pool_w[l].astype(BF16), pool_scale[l], batch, seq, pool_col_block)
        y_sg = _gmlp(z, gmlp_ln_g[l], gmlp_ln_b[l], gmlp_w_s[l].astype(BF16), gmlp_b_s[l], u_block)
        merged = _merge(y_attn, y_pool, y_sg, hd, w_branch, gate_up, gate_b[l], l, tm=1024, tn=1024)
        xf, hu, ssq = _pw_call(_pw_mm_res_norm_kernel, merged, [w_out], l, tm=1024, tn=512, out_dtype=F32,
                               residual=xf, next_gain=ffn_norm_g[l], name="out_proj")
        rs = _row_scale(ssq, d)
        n_wide = d_ff // SWIGLU_WIDE_TN
        act = _pw_call(_pw_swiglu_kernel, hu, [w_ffn_gate, w_ffn_up], l, tm=1024, tn=SWIGLU_WIDE_TN,
                       out_dtype=BF16, row_scale=rs, tiles=(0, n_wide), name="swiglu")
        rest = d_ff - n_wide * SWIGLU_WIDE_TN
        if rest:
            tail_tiles = (n_wide * SWIGLU_WIDE_TN // SWIGLU_TAIL_TN, rest // SWIGLU_TAIL_TN)
            act = _pw_call(_pw_swiglu_into_kernel, hu, [w_ffn_gate, w_ffn_up], l, tm=2048, tn=SWIGLU_TAIL_TN,
                           out_dtype=BF16, row_scale=rs, tiles=tail_tiles, into=act, name="swiglu_tail")
        if l + 1 < depth:
            xf, hu, ssq = _pw_call(_pw_mm_res_norm_kernel, act, [w_ffn_down], l, tm=512, tn=512,
                                   out_dtype=F32, residual=xf, next_gain=attn_norm_g[l + 1],
                                   name="ffn_down")
            rs = _row_scale(ssq, d)
        else:
            xf = _pw_call(_pw_mm_res_kernel, act, [w_ffn_down], l, tm=512, tn=512, out_dtype=F32,
                          residual=xf, name="ffn_down")
    out = _rmsnorm(xf, final_norm_g, F32)
    return out.reshape(batch, seq, d)
```

```python
import functools

import jax
import jax.numpy as jnp
from jax import lax
from jax.experimental import pallas as pl
from jax.experimental.pallas import tpu as pltpu

F32 = jnp.float32
BF16 = jnp.bfloat16

GRID_W = 64
N_HEADS = 16
HEAD_DIM = 128
ATTN_W = N_HEADS * HEAD_DIM
WIN_R = 8
WIN_C = 16
POOL_WINDOWS = (2, 4, 8, 16)
POOL_GROUP = 256
POOL_W = POOL_GROUP * len(POOL_WINDOWS)
GMLP_CHUNK = 128
GMLP_GROUPS = 4
GMLP_GROUP_W = 256
GMLP_W = GMLP_GROUPS * GMLP_GROUP_W
N_BRANCH = 3
EPS = 1e-6

VMEM_LIMIT_BYTES = 62 * 1024 * 1024
LANE = 128

ATTN_Q_ROWS = 4
ATTN_K_ROWS = ATTN_Q_ROWS + WIN_R - 1
SWIGLU_WIDE_TN = 768
SWIGLU_TAIL_TN = 256
POOL_CHUNK = 256
POOL_HALO = 128
MASKED = -1e30


def _params(*sem):
    return pltpu.CompilerParams(dimension_semantics=sem, vmem_limit_bytes=VMEM_LIMIT_BYTES)


def _rmsnorm_kernel(x_ref, g_ref, o_ref):
    x = x_ref[...]
    ms = jnp.mean(x * x, axis=-1, keepdims=True)
    o_ref[...] = (x * lax.rsqrt(ms + EPS) * g_ref[...]).astype(o_ref.dtype)


def _rmsnorm(x, g, out_dtype, tm=256):
    m, d = x.shape
    return pl.pallas_call(
        _rmsnorm_kernel,
        grid=(m // tm,),
        in_specs=[pl.BlockSpec((tm, d), lambda i: (i, 0)),
                  pl.BlockSpec((1, d), lambda i: (0, 0))],
        out_specs=pl.BlockSpec((tm, d), lambda i: (i, 0)),
        out_shape=jax.ShapeDtypeStruct((m, d), out_dtype),
        compiler_params=_params("parallel"),
        name="rmsnorm",
    )(x, g.reshape(1, d))


def _prenorm_kernel(x_ref, g_ref, hu_ref, rs_ref):
    x = x_ref[...]
    ms = jnp.mean(x * x, axis=-1, keepdims=True)
    hu_ref[...] = (x * g_ref[...]).astype(hu_ref.dtype)
    rs_ref[...] = jnp.broadcast_to(lax.rsqrt(ms + EPS), rs_ref.shape)


def _prenorm(x, g, tm=256):
    m, d = x.shape
    return pl.pallas_call(
        _prenorm_kernel,
        grid=(m // tm,),
        in_specs=[pl.BlockSpec((tm, d), lambda i: (i, 0)),
                  pl.BlockSpec((1, d), lambda i: (0, 0))],
        out_specs=[pl.BlockSpec((tm, d), lambda i: (i, 0)),
                   pl.BlockSpec((tm, LANE), lambda i: (i, 0))],
        out_shape=[jax.ShapeDtypeStruct((m, d), BF16), jax.ShapeDtypeStruct((m, LANE), F32)],
        compiler_params=_params("parallel"),
        name="prenorm",
    )(x, g.reshape(1, d))


def _cast_on_first_m_step(pairs):
    @pl.when(pl.program_id(1) == 0)
    def _():
        for src, dst in pairs:
            dst[...] = src[...].astype(BF16)


def _ws_mm_kernel(a_ref, w_ref, rs_ref, o_ref, wb_ref):
    _cast_on_first_m_step([(w_ref, wb_ref)])
    acc = jnp.dot(a_ref[...], wb_ref[...], preferred_element_type=F32)
    o_ref[...] = (acc * rs_ref[:, :1]).astype(o_ref.dtype)


def _matmul_ws(a, w_stack, layer, row_scale, *, tm, tn, out_dtype, name):
    m, k = a.shape
    n = w_stack.shape[2]
    return pl.pallas_call(
        _ws_mm_kernel,
        grid=(n // tn, m // tm),
        in_specs=[pl.BlockSpec((tm, k), lambda j, i: (i, 0)),
                  pl.BlockSpec((None, k, tn), lambda j, i: (layer, 0, j)),
                  pl.BlockSpec((None, tm, LANE), lambda j, i: (row_scale.shape[0] - 1, i, 0))],
        out_specs=pl.BlockSpec((tm, tn), lambda j, i: (i, j)),
        out_shape=jax.ShapeDtypeStruct((m, n), out_dtype),
        scratch_shapes=[pltpu.VMEM((k, tn), BF16)],
        compiler_params=_params("arbitrary", "arbitrary"),
        name=name,
    )(a, w_stack, row_scale)


def _pw_fill(w_refs, slot_refs):
    for w_ref, slot_ref in zip(w_refs, slot_refs):
        rows = w_ref.shape[0]
        r0 = pl.multiple_of(pl.program_id(1) * rows, rows)
        slot_ref[pl.ds(r0, rows), :] = w_ref[...].astype(BF16)


def _pw_phases(w_refs, even_slots, odd_slots, compute, warmup=None):
    j = pl.program_id(0)

    @pl.when(j == 0)
    def _():
        _pw_fill(w_refs, even_slots)
        if warmup is not None:
            warmup()

    @pl.when((j > 0) & (j % 2 == 1))
    def _():
        _pw_fill(w_refs, odd_slots)
        compute(even_slots)

    @pl.when((j > 0) & (j % 2 == 0))
    def _():
        _pw_fill(w_refs, even_slots)
        compute(odd_slots)


def _pw_mm_kernel(a_ref, w_ref, rs_ref, o_ref, w0_ref, w1_ref):
    def compute(slots):
        acc = jnp.dot(a_ref[...], slots[0][...], preferred_element_type=F32)
        o_ref[...] = (acc * rs_ref[:, :1]).astype(o_ref.dtype)
    _pw_phases([w_ref], [w0_ref], [w1_ref], compute)


def _pw_mm_res_kernel(a_ref, w_ref, r_ref, o_ref, w0_ref, w1_ref):
    def compute(slots):
        o_ref[...] = r_ref[...] + jnp.dot(a_ref[...], slots[0][...], preferred_element_type=F32)
    _pw_phases([w_ref], [w0_ref], [w1_ref], compute)


def _pw_mm_res_norm_kernel(a_ref, w_ref, r_ref, g_ref, o_ref, hu_ref, rs_ref, w0_ref, w1_ref, ssq_ref, *, d):
    tm = o_ref.shape[0]
    tile_rows = pl.ds(pl.multiple_of(pl.program_id(1) * tm, tm), tm)

    def warmup():
        ssq_ref[tile_rows, :] = jnp.zeros((tm, LANE), F32)

    def compute(slots):
        x = r_ref[...] + jnp.dot(a_ref[...], slots[0][...], preferred_element_type=F32)
        o_ref[...] = x
        hu_ref[...] = (x * g_ref[...]).astype(hu_ref.dtype)
        ssq = ssq_ref[tile_rows, :] + jnp.sum(x * x, axis=-1, keepdims=True)
        ssq_ref[tile_rows, :] = ssq
        rs_ref[...] = lax.rsqrt(ssq * (1.0 / d) + EPS)
    _pw_phases([w_ref], [w0_ref], [w1_ref], compute, warmup)


def _pw_swiglu_kernel(h_ref, wg_ref, wu_ref, rs_ref, o_ref, g0_ref, u0_ref, g1_ref, u1_ref):
    def compute(slots):
        h = h_ref[...]
        rs = rs_ref[:, :1]
        a = jnp.dot(h, slots[0][...], preferred_element_type=F32) * rs
        b = jnp.dot(h, slots[1][...], preferred_element_type=F32) * rs
        o_ref[...] = (a * (1.0 / (1.0 + jnp.exp(-a))) * b).astype(o_ref.dtype)
    _pw_phases([wg_ref, wu_ref], [g0_ref, u0_ref], [g1_ref, u1_ref], compute)


def _pw_swiglu_into_kernel(h_ref, wg_ref, wu_ref, rs_ref, into_ref, o_ref, g0_ref, u0_ref, g1_ref, u1_ref):
    del into_ref
    _pw_swiglu_kernel(h_ref, wg_ref, wu_ref, rs_ref, o_ref, g0_ref, u0_ref, g1_ref, u1_ref)


def _pw_call(kern, a, w_stacks, layer, *, tm, tn, out_dtype, name, row_scale=None, residual=None,
             next_gain=None, tiles=None, into=None):
    m, k = a.shape
    n = w_stacks[0].shape[2]
    first, n_n = (0, n // tn) if tiles is None else tiles
    assert (first + n_n) * tn <= n and (tiles is not None or n % tn == 0), (n, tn, tiles)
    n_m = m // tm
    rows = k // n_m
    assert k % n_m == 0 and rows % 16 == 0, (k, n_m)
    m_idx = lambda j, i: jnp.where(j == 0, 0, i)
    n_idx = lambda j: first + jnp.maximum(j - 1, 0)
    w_spec = pl.BlockSpec((None, rows, tn), lambda j, i: (layer, jnp.where(j < n_n, i, n_m - 1),
                                                          first + jnp.minimum(j, n_n - 1)))
    o_spec = pl.BlockSpec((tm, tn), lambda j, i: (m_idx(j, i), n_idx(j)))
    in_specs = [pl.BlockSpec((tm, k), lambda j, i: (m_idx(j, i), 0))] + [w_spec] * len(w_stacks)
    args = [a, *w_stacks]
    out_specs = o_spec
    out_shape = jax.ShapeDtypeStruct((m, n), out_dtype)
    aliases = {}
    scratch = [pltpu.VMEM((k, tn), BF16)] * (2 * len(w_stacks))
    if row_scale is not None:
        in_specs.append(pl.BlockSpec((None, tm, LANE),
                                     lambda j, i: (row_scale.shape[0] - 1, m_idx(j, i), 0)))
        args.append(row_scale)
    if residual is not None:
        in_specs.append(o_spec)
        args.append(residual)
    if next_gain is not None:
        in_specs.append(pl.BlockSpec((1, tn), lambda j, i: (0, n_idx(j))))
        args.append(next_gain.reshape(1, n))
        out_specs = [o_spec, o_spec, pl.BlockSpec((None, tm, LANE), lambda j, i: (n_idx(j), m_idx(j, i), 0))]
        out_shape = [out_shape, jax.ShapeDtypeStruct((m, n), BF16),
                     jax.ShapeDtypeStruct((n_n, m, LANE), F32)]
        scratch.append(pltpu.VMEM((m, LANE), F32))
        kern = functools.partial(kern, d=n)
    if into is not None:
        assert next_gain is None and into.shape == (m, n) and into.dtype == out_dtype
        aliases = {len(args): 0}
        in_specs.append(pl.BlockSpec(memory_space=pl.ANY))
        args.append(into)
    return pl.pallas_call(
        kern,
        grid=(n_n + 1, n_m),
        in_specs=in_specs,
        out_specs=out_specs,
        out_shape=out_shape,
        scratch_shapes=scratch,
        input_output_aliases=aliases,
        compiler_params=_params("arbitrary", "arbitrary"),
        name=name,
    )(*args)


def _attn_group_start(g, rows):
    return min(max(g * ATTN_Q_ROWS - WIN_R // 2, 0), rows - ATTN_K_ROWS)


def _attn_row_windows(g, rows):
    start = _attn_group_start(g, rows)
    out = []
    for i in range(ATTN_Q_ROWS):
        qr = g * ATTN_Q_ROWS + i
        rs = min(max(qr - WIN_R // 2, 0), rows - WIN_R)
        out.append(((rs - start) * GRID_W, start - qr + (WIN_R - 1) + ATTN_SRC_LEFT))
    return tuple(out)


def _attn_group_types(rows):
    sigs, idx = [], []
    for g in range(rows // ATTN_Q_ROWS):
        sig = _attn_row_windows(g, rows)
        if sig not in sigs:
            sigs.append(sig)
        idx.append(sigs.index(sig))
    return sigs, idx


ATTN_SRC_LEFT = ATTN_Q_ROWS - 1
ATTN_SRC_W = -(-((WIN_R - 1 + ATTN_SRC_LEFT + ATTN_K_ROWS) * GRID_W) // LANE) * LANE


def _attn_bias_source(rpb):
    nl, h, n_ro, _ = rpb.shape
    rpb = rpb.astype(F32) * (HEAD_DIM ** 0.5)
    qc = jnp.arange(GRID_W)[:, None]
    kc = jnp.arange(GRID_W)[None, :]
    co = jnp.clip(kc - qc, -(WIN_C - 1), WIN_C - 1) + (WIN_C - 1)
    pick = (jnp.arange(2 * WIN_C - 1)[:, None, None] == co[None]).astype(F32)
    toep = jnp.einsum("lhrk,kqc->lhqrc", rpb, pick, precision=lax.Precision.HIGHEST)
    cs = jnp.clip(qc - WIN_C // 2, 0, GRID_W - WIN_C)
    valid = (kc >= cs) & (kc < cs + WIN_C)
    toep = jnp.where(valid[:, None, :], toep, MASKED)
    by_q = toep.reshape(nl, h, GRID_W, n_ro * GRID_W)
    copies = []
    for shift in (0, 1):
        left = (ATTN_SRC_LEFT - shift) * GRID_W
        pad = ((0, 0), (0, 0), (0, 0), (left, ATTN_SRC_W - left - n_ro * GRID_W))
        copies.append(jnp.pad(by_q, pad, constant_values=MASKED))
    return jnp.stack(copies, axis=2)


def _attn_fill_tables(src_ref, tab_ref, rows):
    nk = ATTN_K_ROWS * GRID_W
    lane = lax.broadcasted_iota(jnp.int32, (GRID_W, nk), 1)
    for t, sig in enumerate(_attn_group_types(rows)[0]):
        for i, (lo, u) in enumerate(sig):
            copy = u % 2
            window = src_ref[copy, :, pl.ds((u - copy) * GRID_W, nk)]
            valid = (lane >= lo) & (lane < lo + WIN_R * GRID_W)
            tab_ref[t, pl.ds(i * GRID_W, GRID_W), :] = jnp.where(valid, window, MASKED)


def _attn_kernel(q_ref, k_ref, v_ref, src_ref, o_ref, tab_ref, *, rows):
    n_groups = rows // ATTN_Q_ROWS
    tq = ATTN_Q_ROWS * GRID_W
    nk = ATTN_K_ROWS * GRID_W
    exp2_scale = (HEAD_DIM ** -0.5) * 1.4426950408889634
    _attn_fill_tables(src_ref, tab_ref, rows)
    group_type = _attn_group_types(rows)[1]
    for g in range(n_groups):
        start = _attn_group_start(g, rows) * GRID_W
        q = q_ref[pl.ds(g * tq, tq), :]
        kw = k_ref[pl.ds(start, nk), :]
        vw = v_ref[pl.ds(start, nk), :]
        s = lax.dot_general(q, kw, (((1,), (1,)), ((), ())), preferred_element_type=F32)
        s = s + tab_ref[group_type[g]]
        p = jnp.exp2((s - jnp.max(s, axis=-1, keepdims=True)) * exp2_scale)
        vw_ext = jnp.concatenate([vw, jnp.ones((nk, HEAD_DIM), BF16)], axis=1)
        o = jnp.dot(p.astype(BF16), vw_ext, preferred_element_type=F32)
        o_ref[pl.ds(g * tq, tq), :] = (o[:, :HEAD_DIM] / o[:, HEAD_DIM:HEAD_DIM + 1]).astype(o_ref.dtype)


def _attention(z, bias_src, layer, batch, seq):
    rows = seq // GRID_W
    tq = ATTN_Q_ROWS * GRID_W
    tk = ATTN_K_ROWS * GRID_W
    return pl.pallas_call(
        functools.partial(_attn_kernel, rows=rows),
        grid=(batch, N_HEADS),
        in_specs=[pl.BlockSpec((seq, HEAD_DIM), lambda b, h: (b, h)),
                  pl.BlockSpec((seq, HEAD_DIM), lambda b, h: (b, N_HEADS + h)),
                  pl.BlockSpec((seq, HEAD_DIM), lambda b, h: (b, 2 * N_HEADS + h)),
                  pl.BlockSpec((None, None, 2, GRID_W, ATTN_SRC_W), lambda b, h: (layer, h, 0, 0, 0))],
        out_specs=pl.BlockSpec((seq, HEAD_DIM), lambda b, h: (b, h)),
        out_shape=jax.ShapeDtypeStruct((batch * seq, ATTN_W), BF16),
        scratch_shapes=[pltpu.VMEM((len(_attn_group_types(rows)[0]), tq, tk), F32)],
        compiler_params=_params("parallel", "parallel"),
        name="nbr_attention",
    )(z, z, z, bias_src)


def _pool_band_matrices():
    i = jnp.arange(POOL_CHUNK)[:, None]
    off = jnp.arange(POOL_CHUNK + 2 * POOL_HALO)[None, :] - POOL_HALO - i
    return jnp.stack([((off >= -(w // 2)) & (off <= w // 2 - 1)).astype(BF16) for w in POOL_WINDOWS])


def _pool_kernel(p_ref, band_ref, w_ref, sc_ref, o_ref, pad_ref, *, seq):
    g = pl.program_id(1)
    half = jnp.left_shift(1, g)
    cols = p_ref.shape[1]
    pad_ref[pl.ds(0, POOL_HALO), :] = jnp.zeros((POOL_HALO, cols), BF16)
    pad_ref[pl.ds(POOL_HALO + seq, POOL_HALO), :] = jnp.zeros((POOL_HALO, cols), BF16)
    pad_ref[pl.ds(POOL_HALO, seq), :] = p_ref[...]

    for c in range(seq // POOL_CHUNK):
        base = c * POOL_CHUNK
        ph = pad_ref[pl.ds(base, POOL_CHUNK + 2 * POOL_HALO), :]
        wsum = jnp.dot(band_ref[0], ph, preferred_element_type=F32)
        t = base + lax.broadcasted_iota(jnp.int32, (POOL_CHUNK, 1), 0)
        lo = jnp.maximum(t - half, 0)
        hi = jnp.minimum(t + half - 1, seq - 1)
        cnt = (hi - lo + 1).astype(F32)
        centre = ph[POOL_HALO:POOL_HALO + POOL_CHUNK].astype(F32)
        d = wsum / cnt - centre
        y = jnp.dot(d.astype(BF16), w_ref[0], preferred_element_type=F32) * sc_ref[...]
        o_ref[pl.ds(base, POOL_CHUNK), :] = y.astype(o_ref.dtype)


def _pool(z, bands, w_grp, scale, batch, seq, col_block0):
    n_g = len(POOL_WINDOWS)
    return pl.pallas_call(
        functools.partial(_pool_kernel, seq=seq),
        grid=(batch, n_g),
        in_specs=[pl.BlockSpec((seq, POOL_GROUP), lambda b, g: (b, col_block0 + g)),
                  pl.BlockSpec((1, POOL_CHUNK, POOL_CHUNK + 2 * POOL_HALO), lambda b, g: (g, 0, 0)),
                  pl.BlockSpec((1, POOL_GROUP, POOL_GROUP), lambda b, g: (g, 0, 0)),
                  pl.BlockSpec((1, POOL_GROUP), lambda b, g: (0, g))],
        out_specs=pl.BlockSpec((seq, POOL_GROUP), lambda b, g: (b, g)),
        out_shape=jax.ShapeDtypeStruct((batch * seq, POOL_W), BF16),
        scratch_shapes=[pltpu.VMEM((seq + 2 * POOL_HALO, POOL_GROUP), BF16)],
        compiler_params=_params("parallel", "arbitrary"),
        name="multiscale_pool",
    )(z, bands, w_grp, scale.reshape(1, POOL_W))


def _gelu_tanh(x):
    return 0.5 * x * (1.0 + jnp.tanh(0.7978845608028654 * (x + 0.044715 * (x * x * x))))


def _gmlp_kernel(u_ref, v_ref, lng_ref, lnb_ref, ws_ref, bs_ref, o_ref):
    v = _gelu_tanh(v_ref[...].astype(F32))
    mu = jnp.mean(v, axis=-1, keepdims=True)
    vc = v - mu
    var = jnp.mean(vc * vc, axis=-1, keepdims=True)
    vn = (vc * lax.rsqrt(var + EPS) * lng_ref[...] + lnb_ref[...]).astype(BF16)
    n_chunks = u_ref.shape[0] // GMLP_CHUNK
    for c in range(n_chunks):
        r = slice(c * GMLP_CHUNK, (c + 1) * GMLP_CHUNK)
        for g in range(GMLP_GROUPS):
            cs = slice(g * GMLP_GROUP_W, (g + 1) * GMLP_GROUP_W)
            mixed = jnp.dot(ws_ref[g], vn[r, cs], preferred_element_type=F32) + bs_ref[g]
            u = _gelu_tanh(u_ref[r, cs].astype(F32))
            o_ref[r, cs] = (u * mixed).astype(o_ref.dtype)


def _gmlp(z, ln_g, ln_b, w_s, b_s, u_block, tm=512):
    m = z.shape[0]
    return pl.pallas_call(
        _gmlp_kernel,
        grid=(m // tm,),
        in_specs=[pl.BlockSpec((tm, GMLP_W), lambda i: (i, u_block)),
                  pl.BlockSpec((tm, GMLP_W), lambda i: (i, u_block + 1)),
                  pl.BlockSpec((1, GMLP_W), lambda i: (0, 0)),
                  pl.BlockSpec((1, GMLP_W), lambda i: (0, 0)),
                  pl.BlockSpec((GMLP_GROUPS, GMLP_CHUNK, GMLP_CHUNK), lambda i: (0, 0, 0)),
                  pl.BlockSpec((GMLP_GROUPS, GMLP_CHUNK, 1), lambda i: (0, 0, 0))],
        out_specs=pl.BlockSpec((tm, GMLP_W), lambda i: (i, 0)),
        out_shape=jax.ShapeDtypeStruct((m, GMLP_W), BF16),
        compiler_params=_params("parallel"),
        name="spatial_gating",
    )(z, z, ln_g.reshape(1, GMLP_W), ln_b.reshape(1, GMLP_W), w_s, b_s.reshape(GMLP_GROUPS, GMLP_CHUNK, 1))


def _sigmoid(x):
    return 1.0 / (1.0 + jnp.exp(-x))


def _merge_kernel(ya_ref, yp_ref, ys_ref, hd_ref, wb_ref, gu0_ref, gu1_ref, gu2_ref,
                  gb0_ref, gb1_ref, gb2_ref, o_ref, wb0_ref, gus0_ref, wb1_ref, gus1_ref):
    def compute(slots):
        wbb_ref, gub_ref = slots[0], slots[1]
        hd = hd_ref[...]
        o_pool = ATTN_W
        o_sg = ATTN_W + POOL_W
        ga = _sigmoid(jnp.dot(hd, gub_ref[0], preferred_element_type=F32) + gb0_ref[...])
        acc = ga * jnp.dot(ya_ref[...], wbb_ref[pl.ds(0, ATTN_W), :], preferred_element_type=F32)
        gp = _sigmoid(jnp.dot(hd, gub_ref[1], preferred_element_type=F32) + gb1_ref[...])
        acc = acc + gp * jnp.dot(yp_ref[...], wbb_ref[pl.ds(o_pool, POOL_W), :], preferred_element_type=F32)
        gs = _sigmoid(jnp.dot(hd, gub_ref[2], preferred_element_type=F32) + gb2_ref[...])
        acc = acc + gs * jnp.dot(ys_ref[...], wbb_ref[pl.ds(o_sg, GMLP_W), :], preferred_element_type=F32)
        o_ref[...] = acc.astype(o_ref.dtype)

    def fill_targets(wbb_ref, gub_ref):
        return [wbb_ref, gub_ref.at[0], gub_ref.at[1], gub_ref.at[2]]

    w_refs = [wb_ref, gu0_ref, gu1_ref, gu2_ref]
    j = pl.program_id(0)

    @pl.when(j == 0)
    def _():
        _pw_fill(w_refs, fill_targets(wb0_ref, gus0_ref))

    @pl.when((j > 0) & (j % 2 == 1))
    def _():
        _pw_fill(w_refs, fill_targets(wb1_ref, gus1_ref))
        compute([wb0_ref, gus0_ref])

    @pl.when((j > 0) & (j % 2 == 0))
    def _():
        _pw_fill(w_refs, fill_targets(wb0_ref, gus0_ref))
        compute([wb1_ref, gus1_ref])


def _merge(ya, yp, ys, hd, wb_stack, gate_up_stack, gate_b, layer, *, tm, tn):
    m = ya.shape[0]
    mix_w, d = wb_stack.shape[1:]
    rank = hd.shape[1]
    n_n, n_m = d // tn, m // tm
    assert mix_w % n_m == 0 and rank % n_m == 0 and (rank // n_m) % 16 == 0, (mix_w, rank, n_m)
    m_idx = lambda j, i: jnp.where(j == 0, 0, i)
    n_idx = lambda j: jnp.maximum(j - 1, 0)
    chunk = lambda j, i: jnp.where(j < n_n, i, n_m - 1)
    tile = lambda j: jnp.minimum(j, n_n - 1)
    row = lambda w: pl.BlockSpec((tm, w), lambda j, i: (m_idx(j, i), 0))
    gate_b = gate_b.reshape(1, N_BRANCH * d)
    gu = lambda br: pl.BlockSpec((None, rank // n_m, tn),
                                 lambda j, i: (layer, chunk(j, i), br * n_n + tile(j)))
    gb = lambda br: pl.BlockSpec((1, tn), lambda j, i: (0, br * n_n + n_idx(j)))
    slot_shapes = [pltpu.VMEM((mix_w, tn), BF16), pltpu.VMEM((N_BRANCH, rank, tn), BF16)]
    return pl.pallas_call(
        _merge_kernel,
        grid=(n_n + 1, n_m),
        in_specs=[row(ATTN_W), row(POOL_W), row(GMLP_W), row(rank),
                  pl.BlockSpec((None, mix_w // n_m, tn), lambda j, i: (layer, chunk(j, i), tile(j))),
                  gu(0), gu(1), gu(2), gb(0), gb(1), gb(2)],
        out_specs=pl.BlockSpec((tm, tn), lambda j, i: (m_idx(j, i), n_idx(j))),
        out_shape=jax.ShapeDtypeStruct((m, d), BF16),
        scratch_shapes=slot_shapes + slot_shapes,
        compiler_params=_params("arbitrary", "arbitrary"),
        name="branch_merge",
    )(ya, yp, ys, hd, wb_stack, gate_up_stack, gate_up_stack, gate_up_stack, gate_b, gate_b, gate_b)


def kernel(x, attn_norm_g, w_in, rpb, pool_w, pool_scale, gmlp_ln_g, gmlp_ln_b, gmlp_w_s, gmlp_b_s,
           w_branch, gate_down, gate_up, gate_b, w_out, ffn_norm_g, w_ffn_gate, w_ffn_up, w_ffn_down,
           final_norm_g):
    batch, seq, d = x.shape
    depth = w_in.shape[0]
    d_ff = w_ffn_gate.shape[2]
    xf = x.reshape(batch * seq, d)
    bands = _pool_band_matrices()
    pool_col_block = (3 * ATTN_W) // POOL_GROUP
    u_block = (3 * ATTN_W + POOL_W) // GMLP_W
    bias_src = _attn_bias_source(rpb)
    hu, rs = _prenorm(xf, attn_norm_g[0])
    rs = rs[None]
    for l in range(depth):
        z = _pw_call(_pw_mm_kernel, hu, [w_in], l, tm=1024, tn=1024, out_dtype=BF16, row_scale=rs,
                     name="in_proj")
        hd = _matmul_ws(hu, gate_down, l, rs, tm=1024, tn=gate_down.shape[2], out_dtype=BF16,
                        name="gate_down")
        y_attn = _attention(z, bias_src, l, batch, seq)
        y_pool = _pool(z, bands, pool_w[l].astype(BF16), pool_scale[l], batch, seq, pool_col_block)
        y_sg = _gmlp(z, gmlp_ln_g[l], gmlp_ln_b[l], gmlp_w_s[l].astype(BF16), gmlp_b_s[l], u_block)
        merged = _merge(y_attn, y_pool, y_sg, hd, w_branch, gate_up, gate_b[l], l, tm=1024, tn=1024)
        xf, hu, rs = _pw_call(_pw_mm_res_norm_kernel, merged, [w_out], l, tm=1024, tn=512, out_dtype=F32,
                              residual=xf, next_gain=ffn_norm_g[l], name="out_proj")
        n_wide = d_ff // SWIGLU_WIDE_TN
        act = _pw_call(_pw_swiglu_kernel, hu, [w_ffn_gate, w_ffn_up], l, tm=1024, tn=SWIGLU_WIDE_TN,
                       out_dtype=BF16, row_scale=rs, tiles=(0, n_wide), name="swiglu")
        rest = d_ff - n_wide * SWIGLU_WIDE_TN
        if rest:
            tail_tiles = (n_wide * SWIGLU_WIDE_TN // SWIGLU_TAIL_TN, rest // SWIGLU_TAIL_TN)
            act = _pw_call(_pw_swiglu_into_kernel, hu, [w_ffn_gate, w_ffn_up], l, tm=2048, tn=SWIGLU_TAIL_TN,
                           out_dtype=BF16, row_scale=rs, tiles=tail_tiles, into=act, name="swiglu_tail")
        if l + 1 < depth:
            xf, hu, rs = _pw_call(_pw_mm_res_norm_kernel, act, [w_ffn_down], l, tm=512, tn=512,
                                  out_dtype=F32, residual=xf, next_gain=attn_norm_g[l + 1],
                                  name="ffn_down")
        else:
            xf = _pw_call(_pw_mm_res_kernel, act, [w_ffn_down], l, tm=512, tn=512, out_dtype=F32,
                          residual=xf, name="ffn_down")
    out = _rmsnorm(xf, final_norm_g, F32)
    return out.reshape(batch, seq, d)
```

```python
import functools

import jax
import jax.numpy as jnp
from jax import lax
from jax.experimental import pallas as pl
from jax.experimental.pallas import tpu as pltpu

F32 = jnp.float32
BF16 = jnp.bfloat16

GRID_W = 64
N_HEADS = 16
HEAD_DIM = 128
ATTN_W = N_HEADS * HEAD_DIM
WIN_R = 8
WIN_C = 16
POOL_WINDOWS = (2, 4, 8, 16)
POOL_GROUP = 256
POOL_W = POOL_GROUP * len(POOL_WINDOWS)
GMLP_CHUNK = 128
GMLP_GROUPS = 4
GMLP_GROUP_W = 256
GMLP_W = GMLP_GROUPS * GMLP_GROUP_W
N_BRANCH = 3
EPS = 1e-6

VMEM_LIMIT_BYTES = 62 * 1024 * 1024
LANE = 128

ATTN_Q_ROWS = 4
ATTN_K_ROWS = ATTN_Q_ROWS + WIN_R - 1
SWIGLU_WIDE_TN = 768
SWIGLU_TAIL_TN = 256
POOL_CHUNK = 256
POOL_HALO = 128
MASKED = -1e30


def _params(*sem):
    return pltpu.CompilerParams(dimension_semantics=sem, vmem_limit_bytes=VMEM_LIMIT_BYTES)


def _rmsnorm_kernel(x_ref, g_ref, o_ref):
    x = x_ref[...]
    ms = jnp.mean(x * x, axis=-1, keepdims=True)
    o_ref[...] = (x * lax.rsqrt(ms + EPS) * g_ref[...]).astype(o_ref.dtype)


def _rmsnorm(x, g, out_dtype, tm=256):
    m, d = x.shape
    return pl.pallas_call(
        _rmsnorm_kernel,
        grid=(m // tm,),
        in_specs=[pl.BlockSpec((tm, d), lambda i: (i, 0)),
                  pl.BlockSpec((1, d), lambda i: (0, 0))],
        out_specs=pl.BlockSpec((tm, d), lambda i: (i, 0)),
        out_shape=jax.ShapeDtypeStruct((m, d), out_dtype),
        compiler_params=_params("parallel"),
        name="rmsnorm",
    )(x, g.reshape(1, d))


def _prenorm_kernel(x_ref, g_ref, hu_ref, rs_ref):
    x = x_ref[...]
    ms = jnp.mean(x * x, axis=-1, keepdims=True)
    hu_ref[...] = (x * g_ref[...]).astype(hu_ref.dtype)
    rs_ref[...] = jnp.broadcast_to(lax.rsqrt(ms + EPS), rs_ref.shape)


def _prenorm(x, g, tm=256):
    m, d = x.shape
    return pl.pallas_call(
        _prenorm_kernel,
        grid=(m // tm,),
        in_specs=[pl.BlockSpec((tm, d), lambda i: (i, 0)),
                  pl.BlockSpec((1, d), lambda i: (0, 0))],
        out_specs=[pl.BlockSpec((tm, d), lambda i: (i, 0)),
                   pl.BlockSpec((tm, LANE), lambda i: (i, 0))],
        out_shape=[jax.ShapeDtypeStruct((m, d), BF16), jax.ShapeDtypeStruct((m, LANE), F32)],
        compiler_params=_params("parallel"),
        name="prenorm",
    )(x, g.reshape(1, d))


def _cast_on_first_m_step(pairs):
    @pl.when(pl.program_id(1) == 0)
    def _():
        for src, dst in pairs:
            dst[...] = src[...].astype(BF16)


def _ws_mm_kernel(a_ref, w_ref, rs_ref, o_ref, wb_ref):
    _cast_on_first_m_step([(w_ref, wb_ref)])
    acc = jnp.dot(a_ref[...], wb_ref[...], preferred_element_type=F32)
    o_ref[...] = (acc * rs_ref[:, :1]).astype(o_ref.dtype)


def _matmul_ws(a, w_stack, layer, row_scale, *, tm, tn, out_dtype, name):
    m, k = a.shape
    n = w_stack.shape[2]
    return pl.pallas_call(
        _ws_mm_kernel,
        grid=(n // tn, m // tm),
        in_specs=[pl.BlockSpec((tm, k), lambda j, i: (i, 0)),
                  pl.BlockSpec((None, k, tn), lambda j, i: (layer, 0, j)),
                  pl.BlockSpec((None, tm, LANE), lambda j, i: (row_scale.shape[0] - 1, i, 0))],
        out_specs=pl.BlockSpec((tm, tn), lambda j, i: (i, j)),
        out_shape=jax.ShapeDtypeStruct((m, n), out_dtype),
        scratch_shapes=[pltpu.VMEM((k, tn), BF16)],
        compiler_params=_params("arbitrary", "arbitrary"),
        name=name,
    )(a, w_stack, row_scale)


def _pw_fill(w_refs, slot_refs):
    for w_ref, slot_ref in zip(w_refs, slot_refs):
        rows = w_ref.shape[0]
        r0 = pl.multiple_of(pl.program_id(1) * rows, rows)
        slot_ref[pl.ds(r0, rows), :] = w_ref[...].astype(BF16)


def _pw_phases(w_refs, even_slots, odd_slots, compute, warmup=None):
    j = pl.program_id(0)

    @pl.when(j == 0)
    def _():
        _pw_fill(w_refs, even_slots)
        if warmup is not None:
            warmup()

    @pl.when((j > 0) & (j % 2 == 1))
    def _():
        _pw_fill(w_refs, odd_slots)
        compute(even_slots)

    @pl.when((j > 0) & (j % 2 == 0))
    def _():
        _pw_fill(w_refs, even_slots)
        compute(odd_slots)


def _pw_mm_kernel(a_ref, w_ref, rs_ref, o_ref, w0_ref, w1_ref):
    def compute(slots):
        acc = jnp.dot(a_ref[...], slots[0][...], preferred_element_type=F32)
        o_ref[...] = (acc * rs_ref[:, :1]).astype(o_ref.dtype)
    _pw_phases([w_ref], [w0_ref], [w1_ref], compute)


def _pw_mm_res_kernel(a_ref, w_ref, r_ref, o_ref, w0_ref, w1_ref):
    def compute(slots):
        o_ref[...] = r_ref[...] + jnp.dot(a_ref[...], slots[0][...], preferred_element_type=F32)
    _pw_phases([w_ref], [w0_ref], [w1_ref], compute)


def _pw_mm_res_norm_kernel(a_ref, w_ref, r_ref, g_ref, o_ref, hu_ref, rs_ref, w0_ref, w1_ref, ssq_ref, *, d):
    tm = o_ref.shape[0]
    tile_rows = pl.ds(pl.multiple_of(pl.program_id(1) * tm, tm), tm)

    def warmup():
        ssq_ref[tile_rows, :] = jnp.zeros((tm, LANE), F32)

    def compute(slots):
        x = r_ref[...] + jnp.dot(a_ref[...], slots[0][...], preferred_element_type=F32)
        o_ref[...] = x
        hu_ref[...] = (x * g_ref[...]).astype(hu_ref.dtype)
        ssq = ssq_ref[tile_rows, :] + jnp.sum(x * x, axis=-1, keepdims=True)
        ssq_ref[tile_rows, :] = ssq
        rs_ref[...] = lax.rsqrt(ssq * (1.0 / d) + EPS)
    _pw_phases([w_ref], [w0_ref], [w1_ref], compute, warmup)


def _pw_swiglu_kernel(h_ref, wg_ref, wu_ref, rs_ref, o_ref, g0_ref, u0_ref, g1_ref, u1_ref):
    def compute(slots):
        h = h_ref[...]
        rs = rs_ref[:, :1]
        a = jnp.dot(h, slots[0][...], preferred_element_type=F32) * rs
        b = jnp.dot(h, slots[1][...], preferred_element_type=F32) * rs
        o_ref[...] = (a * (1.0 / (1.0 + jnp.exp(-a))) * b).astype(o_ref.dtype)
    _pw_phases([wg_ref, wu_ref], [g0_ref, u0_ref], [g1_ref, u1_ref], compute)


def _pw_swiglu_into_kernel(h_ref, wg_ref, wu_ref, rs_ref, into_ref, o_ref, g0_ref, u0_ref, g1_ref, u1_ref):
    del into_ref
    _pw_swiglu_kernel(h_ref, wg_ref, wu_ref, rs_ref, o_ref, g0_ref, u0_ref, g1_ref, u1_ref)


def _pw_call(kern, a, w_stacks, layer, *, tm, tn, out_dtype, name, row_scale=None, residual=None,
             next_gain=None, tiles=None, into=None):
    m, k = a.shape
    n = w_stacks[0].shape[2]
    first, n_n = (0, n // tn) if tiles is None else tiles
    assert (first + n_n) * tn <= n and (tiles is not None or n % tn == 0), (n, tn, tiles)
    n_m = m // tm
    rows = k // n_m
    assert k % n_m == 0 and rows % 16 == 0, (k, n_m)
    m_idx = lambda j, i: jnp.where(j == 0, 0, i)
    n_idx = lambda j: first + jnp.maximum(j - 1, 0)
    w_spec = pl.BlockSpec((None, rows, tn), lambda j, i: (layer, jnp.where(j < n_n, i, n_m - 1),
                                                          first + jnp.minimum(j, n_n - 1)))
    o_spec = pl.BlockSpec((tm, tn), lambda j, i: (m_idx(j, i), n_idx(j)))
    in_specs = [pl.BlockSpec((tm, k), lambda j, i: (m_idx(j, i), 0))] + [w_spec] * len(w_stacks)
    args = [a, *w_stacks]
    out_specs = o_spec
    out_shape = jax.ShapeDtypeStruct((m, n), out_dtype)
    aliases = {}
    scratch = [pltpu.VMEM((k, tn), BF16)] * (2 * len(w_stacks))
    if row_scale is not None:
        in_specs.append(pl.BlockSpec((None, tm, LANE),
                                     lambda j, i: (row_scale.shape[0] - 1, m_idx(j, i), 0)))
        args.append(row_scale)
    if residual is not None:
        in_specs.append(o_spec)
        args.append(residual)
    if next_gain is not None:
        in_specs.append(pl.BlockSpec((1, tn), lambda j, i: (0, n_idx(j))))
        args.append(next_gain.reshape(1, n))
        out_specs = [o_spec, o_spec, pl.BlockSpec((None, tm, LANE), lambda j, i: (n_idx(j), m_idx(j, i), 0))]
        out_shape = [out_shape, jax.ShapeDtypeStruct((m, n), BF16),
                     jax.ShapeDtypeStruct((n_n, m, LANE), F32)]
        scratch.append(pltpu.VMEM((m, LANE), F32))
        kern = functools.partial(kern, d=n)
    if into is not None:
        assert next_gain is None and into.shape == (m, n) and into.dtype == out_dtype
        aliases = {len(args): 0}
        in_specs.append(pl.BlockSpec(memory_space=pl.ANY))
        args.append(into)
    return pl.pallas_call(
        kern,
        grid=(n_n + 1, n_m),
        in_specs=in_specs,
        out_specs=out_specs,
        out_shape=out_shape,
        scratch_shapes=scratch,
        input_output_aliases=aliases,
        compiler_params=_params("arbitrary", "arbitrary"),
        name=name,
    )(*args)


def _attn_group_start(g, rows):
    return min(max(g * ATTN_Q_ROWS - WIN_R // 2, 0), rows - ATTN_K_ROWS)


def _attn_row_windows(g, rows):
    start = _attn_group_start(g, rows)
    out = []
    for i in range(ATTN_Q_ROWS):
        qr = g * ATTN_Q_ROWS + i
        rs = min(max(qr - WIN_R // 2, 0), rows - WIN_R)
        out.append(((rs - start) * GRID_W, start - qr + (WIN_R - 1) + ATTN_SRC_LEFT))
    return tuple(out)


def _attn_group_types(rows):
    sigs, idx = [], []
    for g in range(rows // ATTN_Q_ROWS):
        sig = _attn_row_windows(g, rows)
        if sig not in sigs:
            sigs.append(sig)
        idx.append(sigs.index(sig))
    return sigs, idx


ATTN_SRC_LEFT = ATTN_Q_ROWS - 1
ATTN_SRC_W = -(-((WIN_R - 1 + ATTN_SRC_LEFT + ATTN_K_ROWS) * GRID_W) // LANE) * LANE


def _attn_bias_source(rpb):
    nl, h, n_ro, _ = rpb.shape
    rpb = rpb.astype(F32) * (HEAD_DIM ** 0.5)
    qc = jnp.arange(GRID_W)[:, None]
    kc = jnp.arange(GRID_W)[None, :]
    co = jnp.clip(kc - qc, -(WIN_C - 1), WIN_C - 1) + (WIN_C - 1)
    pick = (jnp.arange(2 * WIN_C - 1)[:, None, None] == co[None]).astype(F32)
    toep = jnp.einsum("lhrk,kqc->lhqrc", rpb, pick, precision=lax.Precision.HIGHEST)
    cs = jnp.clip(qc - WIN_C // 2, 0, GRID_W - WIN_C)
    valid = (kc >= cs) & (kc < cs + WIN_C)
    toep = jnp.where(valid[:, None, :], toep, MASKED)
    by_q = toep.reshape(nl, h, GRID_W, n_ro * GRID_W)
    copies = []
    for shift in (0, 1):
        left = (ATTN_SRC_LEFT - shift) * GRID_W
        pad = ((0, 0), (0, 0), (0, 0), (left, ATTN_SRC_W - left - n_ro * GRID_W))
        copies.append(jnp.pad(by_q, pad, constant_values=MASKED))
    return jnp.stack(copies, axis=2)


def _attn_fill_tables(src_ref, tab_ref, rows):
    nk = ATTN_K_ROWS * GRID_W
    lane = lax.broadcasted_iota(jnp.int32, (GRID_W, nk), 1)
    for t, sig in enumerate(_attn_group_types(rows)[0]):
        for i, (lo, u) in enumerate(sig):
            copy = u % 2
            window = src_ref[copy, :, pl.ds((u - copy) * GRID_W, nk)]
            valid = (lane >= lo) & (lane < lo + WIN_R * GRID_W)
            tab_ref[t, pl.ds(i * GRID_W, GRID_W), :] = jnp.where(valid, window, MASKED)


def _attn_kernel(q_ref, k_ref, v_ref, src_ref, o_ref, tab_ref, *, rows):
    n_groups = rows // ATTN_Q_ROWS
    tq = ATTN_Q_ROWS * GRID_W
    nk = ATTN_K_ROWS * GRID_W
    exp2_scale = (HEAD_DIM ** -0.5) * 1.4426950408889634
    _attn_fill_tables(src_ref, tab_ref, rows)
    group_type = _attn_group_types(rows)[1]
    for g in range(n_groups):
        start = _attn_group_start(g, rows) * GRID_W
        q = q_ref[pl.ds(g * tq, tq), :]
        kw = k_ref[pl.ds(start, nk), :]
        vw = v_ref[pl.ds(start, nk), :]
        s = lax.dot_general(q, kw, (((1,), (1,)), ((), ())), preferred_element_type=F32)
        s = s + tab_ref[group_type[g]]
        p = jnp.exp2((s - jnp.max(s, axis=-1, keepdims=True)) * exp2_scale)
        vw_ext = jnp.concatenate([vw, jnp.ones((nk, HEAD_DIM), BF16)], axis=1)
        o = jnp.dot(p.astype(BF16), vw_ext, preferred_element_type=F32)
        o_ref[pl.ds(g * tq, tq), :] = (o[:, :HEAD_DIM] / o[:, HEAD_DIM:HEAD_DIM + 1]).astype(o_ref.dtype)


def _attention(z, bias_src, layer, batch, seq):
    rows = seq // GRID_W
    tq = ATTN_Q_ROWS * GRID_W
    tk = ATTN_K_ROWS * GRID_W
    return pl.pallas_call(
        functools.partial(_attn_kernel, rows=rows),
        grid=(batch, N_HEADS),
        in_specs=[pl.BlockSpec((seq, HEAD_DIM), lambda b, h: (b, h)),
                  pl.BlockSpec((seq, HEAD_DIM), lambda b, h: (b, N_HEADS + h)),
                  pl.BlockSpec((seq, HEAD_DIM), lambda b, h: (b, 2 * N_HEADS + h)),
                  pl.BlockSpec((None, None, 2, GRID_W, ATTN_SRC_W), lambda b, h: (layer, h, 0, 0, 0))],
        out_specs=pl.BlockSpec((seq, HEAD_DIM), lambda b, h: (b, h)),
        out_shape=jax.ShapeDtypeStruct((batch * seq, ATTN_W), BF16),
        scratch_shapes=[pltpu.VMEM((len(_attn_group_types(rows)[0]), tq, tk), F32)],
        compiler_params=_params("parallel", "parallel"),
        name="nbr_attention",
    )(z, z, z, bias_src)


def _pool_band_matrices():
    i = jnp.arange(POOL_CHUNK)[:, None]
    off = jnp.arange(POOL_CHUNK + 2 * POOL_HALO)[None, :] - POOL_HALO - i
    return jnp.stack([((off >= -(w // 2)) & (off <= w // 2 - 1)).astype(BF16) for w in POOL_WINDOWS])


def _pool_kernel(p_ref, band_ref, w_ref, sc_ref, o_ref, pad_ref, *, seq):
    g = pl.program_id(1)
    half = jnp.left_shift(1, g)
    cols = p_ref.shape[1]
    pad_ref[pl.ds(0, POOL_HALO), :] = jnp.zeros((POOL_HALO, cols), BF16)
    pad_ref[pl.ds(POOL_HALO + seq, POOL_HALO), :] = jnp.zeros((POOL_HALO, cols), BF16)
    pad_ref[pl.ds(POOL_HALO, seq), :] = p_ref[...]

    for c in range(seq // POOL_CHUNK):
        base = c * POOL_CHUNK
        ph = pad_ref[pl.ds(base, POOL_CHUNK + 2 * POOL_HALO), :]
        wsum = jnp.dot(band_ref[0], ph, preferred_element_type=F32)
        t = base + lax.broadcasted_iota(jnp.int32, (POOL_CHUNK, 1), 0)
        lo = jnp.maximum(t - half, 0)
        hi = jnp.minimum(t + half - 1, seq - 1)
        cnt = (hi - lo + 1).astype(F32)
        centre = ph[POOL_HALO:POOL_HALO + POOL_CHUNK].astype(F32)
        d = wsum / cnt - centre
        y = jnp.dot(d.astype(BF16), w_ref[0], preferred_element_type=F32) * sc_ref[...]
        o_ref[pl.ds(base, POOL_CHUNK), :] = y.astype(o_ref.dtype)


def _pool(z, bands, w_grp, scale, batch, seq, col_block0):
    n_g = len(POOL_WINDOWS)
    return pl.pallas_call(
        functools.partial(_pool_kernel, seq=seq),
        grid=(batch, n_g),
        in_specs=[pl.BlockSpec((seq, POOL_GROUP), lambda b, g: (b, col_block0 + g)),
                  pl.BlockSpec((1, POOL_CHUNK, POOL_CHUNK + 2 * POOL_HALO), lambda b, g: (g, 0, 0)),
                  pl.BlockSpec((1, POOL_GROUP, POOL_GROUP), lambda b, g: (g, 0, 0)),
                  pl.BlockSpec((1, POOL_GROUP), lambda b, g: (0, g))],
        out_specs=pl.BlockSpec((seq, POOL_GROUP), lambda b, g: (b, g)),
        out_shape=jax.ShapeDtypeStruct((batch * seq, POOL_W), BF16),
        scratch_shapes=[pltpu.VMEM((seq + 2 * POOL_HALO, POOL_GROUP), BF16)],
        compiler_params=_params("parallel", "arbitrary"),
        name="multiscale_pool",
    )(z, bands, w_grp, scale.reshape(1, POOL_W))


def _gelu_tanh(x):
    return 0.5 * x * (1.0 + jnp.tanh(0.7978845608028654 * (x + 0.044715 * (x * x * x))))


def _gmlp_kernel(u_ref, v_ref, lng_ref, lnb_ref, ws_ref, bs_ref, o_ref):
    v = _gelu_tanh(v_ref[...].astype(F32))
    mu = jnp.mean(v, axis=-1, keepdims=True)
    vc = v - mu
    var = jnp.mean(vc * vc, axis=-1, keepdims=True)
    vn = (vc * lax.rsqrt(var + EPS) * lng_ref[...] + lnb_ref[...]).astype(BF16)
    n_chunks = u_ref.shape[0] // GMLP_CHUNK
    for c in range(n_chunks):
        r = slice(c * GMLP_CHUNK, (c + 1) * GMLP_CHUNK)
        for g in range(GMLP_GROUPS):
            cs = slice(g * GMLP_GROUP_W, (g + 1) * GMLP_GROUP_W)
            mixed = jnp.dot(ws_ref[g], vn[r, cs], preferred_element_type=F32) + bs_ref[g]
            u = _gelu_tanh(u_ref[r, cs].astype(F32))
            o_ref[r, cs] = (u * mixed).astype(o_ref.dtype)


def _gmlp(z, ln_g, ln_b, w_s, b_s, u_block, tm=512):
    m = z.shape[0]
    return pl.pallas_call(
        _gmlp_kernel,
        grid=(m // tm,),
        in_specs=[pl.BlockSpec((tm, GMLP_W), lambda i: (i, u_block)),
                  pl.BlockSpec((tm, GMLP_W), lambda i: (i, u_block + 1)),
                  pl.BlockSpec((1, GMLP_W), lambda i: (0, 0)),
                  pl.BlockSpec((1, GMLP_W), lambda i: (0, 0)),
                  pl.BlockSpec((GMLP_GROUPS, GMLP_CHUNK, GMLP_CHUNK), lambda i: (0, 0, 0)),
                  pl.BlockSpec((GMLP_GROUPS, GMLP_CHUNK, 1), lambda i: (0, 0, 0))],
        out_specs=pl.BlockSpec((tm, GMLP_W), lambda i: (i, 0)),
        out_shape=jax.ShapeDtypeStruct((m, GMLP_W), BF16),
        compiler_params=_params("parallel"),
        name="spatial_gating",
    )(z, z, ln_g.reshape(1, GMLP_W), ln_b.reshape(1, GMLP_W), w_s, b_s.reshape(GMLP_GROUPS, GMLP_CHUNK, 1))


def _sigmoid(x):
    return 1.0 / (1.0 + jnp.exp(-x))


def _merge_kernel(ya_ref, yp_ref, ys_ref, hd_ref, wb_ref, gu0_ref, gu1_ref, gu2_ref,
                  gb0_ref, gb1_ref, gb2_ref, o_ref, wb0_ref, gus0_ref, wb1_ref, gus1_ref):
    def compute(slots):
        wbb_ref, gub_ref = slots[0], slots[1]
        hd = hd_ref[...]
        o_pool = ATTN_W
        o_sg = ATTN_W + POOL_W
        ga = _sigmoid(jnp.dot(hd, gub_ref[0], preferred_element_type=F32) + gb0_ref[...])
        acc = ga * jnp.dot(ya_ref[...], wbb_ref[pl.ds(0, ATTN_W), :], preferred_element_type=F32)
        gp = _sigmoid(jnp.dot(hd, gub_ref[1], preferred_element_type=F32) + gb1_ref[...])
        acc = acc + gp * jnp.dot(yp_ref[...], wbb_ref[pl.ds(o_pool, POOL_W), :], preferred_element_type=F32)
        gs = _sigmoid(jnp.dot(hd, gub_ref[2], preferred_element_type=F32) + gb2_ref[...])
        acc = acc + gs * jnp.dot(ys_ref[...], wbb_ref[pl.ds(o_sg, GMLP_W), :], preferred_element_type=F32)
        o_ref[...] = acc.astype(o_ref.dtype)

    def fill_targets(wbb_ref, gub_ref):
        return [wbb_ref, gub_ref.at[0], gub_ref.at[1], gub_ref.at[2]]

    w_refs = [wb_ref, gu0_ref, gu1_ref, gu2_ref]
    j = pl.program_id(0)

    @pl.when(j == 0)
    def _():
        _pw_fill(w_refs, fill_targets(wb0_ref, gus0_ref))

    @pl.when((j > 0) & (j % 2 == 1))
    def _():
        _pw_fill(w_refs, fill_targets(wb1_ref, gus1_ref))
        compute([wb0_ref, gus0_ref])

    @pl.when((j > 0) & (j % 2 == 0))
    def _():
        _pw_fill(w_refs, fill_targets(wb0_ref, gus0_ref))
        compute([wb1_ref, gus1_ref])


def _merge(ya, yp, ys, hd, wb_stack, gate_up_stack, gate_b, layer, *, tm, tn):
    m = ya.shape[0]
    mix_w, d = wb_stack.shape[1:]
    rank = hd.shape[1]
    n_n, n_m = d // tn, m // tm
    assert mix_w % n_m == 0 and rank % n_m == 0 and (rank // n_m) % 16 == 0, (mix_w, rank, n_m)
    m_idx = lambda j, i: jnp.where(j == 0, 0, i)
    n_idx = lambda j: jnp.maximum(j - 1, 0)
    chunk = lambda j, i: jnp.where(j < n_n, i, n_m - 1)
    tile = lambda j: jnp.minimum(j, n_n - 1)
    row = lambda w: pl.BlockSpec((tm, w), lambda j, i: (m_idx(j, i), 0))
    gate_b = gate_b.reshape(1, N_BRANCH * d)
    gu = lambda br: pl.BlockSpec((None, rank // n_m, tn),
                                 lambda j, i: (layer, chunk(j, i), br * n_n + tile(j)))
    gb = lambda br: pl.BlockSpec((1, tn), lambda j, i: (0, br * n_n + n_idx(j)))
    slot_shapes = [pltpu.VMEM((mix_w, tn), BF16), pltpu.VMEM((N_BRANCH, rank, tn), BF16)]
    return pl.pallas_call(
        _merge_kernel,
        grid=(n_n + 1, n_m),
        in_specs=[row(ATTN_W), row(POOL_W), row(GMLP_W), row(rank),
                  pl.BlockSpec((None, mix_w // n_m, tn), lambda j, i: (layer, chunk(j, i), tile(j))),
                  gu(0), gu(1), gu(2), gb(0), gb(1), gb(2)],
        out_specs=pl.BlockSpec((tm, tn), lambda j, i: (m_idx(j, i), n_idx(j))),
        out_shape=jax.ShapeDtypeStruct((m, d), BF16),
        scratch_shapes=slot_shapes + slot_shapes,
        compiler_params=_params("arbitrary", "arbitrary"),
        name="branch_merge",
    )(ya, yp, ys, hd, wb_stack, gate_up_stack, gate_up_stack, gate_up_stack, gate_b, gate_b, gate_b)


def kernel(x, attn_norm_g, w_in, rpb, pool_w, pool_scale, gmlp_ln_g, gmlp_ln_b, gmlp_w_s, gmlp_b_s,
           w_branch, gate_down, gate_up, gate_b, w_out, ffn_norm_g, w_ffn_gate, w_ffn_up, w_ffn_down,
           final_norm_g):
    batch, seq, d = x.shape
    depth = w_in.shape[0]
    d_ff = w_ffn_gate.shape[2]
    xf = x.reshape(batch * seq, d)
    bands = _pool_band_matrices()
    pool_col_block = (3 * ATTN_W) // POOL_GROUP
    u_block = (3 * ATTN_W + POOL_W) // GMLP_W
    bias_src = _attn_bias_source(rpb)
    hu, rs = _prenorm(xf, attn_norm_g[0])
    rs = rs[None]
    for l in range(depth):
        z = _pw_call(_pw_mm_kernel, hu, [w_in], l, tm=1024, tn=1024, out_dtype=BF16, row_scale=rs,
                     name="in_proj")
        hd = _matmul_ws(hu, gate_down, l, rs, tm=1024, tn=gate_down.shape[2], out_dtype=BF16,
                        name="gate_down")
        y_attn = _attention(z, bias_src, l, batch, seq)
        y_pool = _pool(z, bands, pool_w[l].astype(BF16), pool_scale[l], batch, seq, pool_col_block)
        y_sg = _gmlp(z, gmlp_ln_g[l], gmlp_ln_b[l], gmlp_w_s[l].astype(BF16), gmlp_b_s[l], u_block)
        merged = _merge(y_attn, y_pool, y_sg, hd, w_branch, gate_up, gate_b[l], l, tm=1024, tn=1024)
        xf, hu, rs = _pw_call(_pw_mm_res_norm_kernel, merged, [w_out], l, tm=512, tn=1024, out_dtype=F32,
                              residual=xf, next_gain=ffn_norm_g[l], name="out_proj")
        n_wide = d_ff // SWIGLU_WIDE_TN
        act = _pw_call(_pw_swiglu_kernel, hu, [w_ffn_gate, w_ffn_up], l, tm=1024, tn=SWIGLU_WIDE_TN,
                       out_dtype=BF16, row_scale=rs, tiles=(0, n_wide), name="swiglu")
        rest = d_ff - n_wide * SWIGLU_WIDE_TN
        if rest:
            tail_tiles = (n_wide * SWIGLU_WIDE_TN // SWIGLU_TAIL_TN, rest // SWIGLU_TAIL_TN)
            act = _pw_call(_pw_swiglu_into_kernel, hu, [w_ffn_gate, w_ffn_up], l, tm=2048, tn=SWIGLU_TAIL_TN,
                           out_dtype=BF16, row_scale=rs, tiles=tail_tiles, into=act, name="swiglu_tail")
        if l + 1 < depth:
            xf, hu, rs = _pw_call(_pw_mm_res_norm_kernel, act, [w_ffn_down], l, tm=512, tn=512,
                                  out_dtype=F32, residual=xf, next_gain=attn_norm_g[l + 1],
                                  name="ffn_down")
        else:
            xf = _pw_call(_pw_mm_res_kernel, act, [w_ffn_down], l, tm=512, tn=512, out_dtype=F32,
                          residual=xf, name="ffn_down")
    out = _rmsnorm(xf, final_norm_g, F32)
    return out.reshape(batch, seq, d)
```

```python
import functools

import jax
import jax.numpy as jnp
from jax import lax
from jax.experimental import pallas as pl
from jax.experimental.pallas import tpu as pltpu

F32 = jnp.float32
BF16 = jnp.bfloat16

GRID_W = 64
N_HEADS = 16
HEAD_DIM = 128
ATTN_W = N_HEADS * HEAD_DIM
WIN_R = 8
WIN_C = 16
POOL_WINDOWS = (2, 4, 8, 16)
POOL_GROUP = 256
POOL_W = POOL_GROUP * len(POOL_WINDOWS)
GMLP_CHUNK = 128
GMLP_GROUPS = 4
GMLP_GROUP_W = 256
GMLP_W = GMLP_GROUPS * GMLP_GROUP_W
N_BRANCH = 3
EPS = 1e-6

VMEM_LIMIT_BYTES = 62 * 1024 * 1024
LANE = 128

ATTN_Q_ROWS = 4
ATTN_K_ROWS = ATTN_Q_ROWS + WIN_R - 1
SWIGLU_WIDE_TN = 768
SWIGLU_TAIL_TN = 256
POOL_CHUNK = 256
POOL_HALO = 128
MASKED = -1e30


def _params(*sem):
    return pltpu.CompilerParams(dimension_semantics=sem, vmem_limit_bytes=VMEM_LIMIT_BYTES)


def _rmsnorm_kernel(x_ref, g_ref, o_ref):
    x = x_ref[...]
    ms = jnp.mean(x * x, axis=-1, keepdims=True)
    o_ref[...] = (x * lax.rsqrt(ms + EPS) * g_ref[...]).astype(o_ref.dtype)


def _rmsnorm(x, g, out_dtype, tm=256):
    m, d = x.shape
    return pl.pallas_call(
        _rmsnorm_kernel,
        grid=(m // tm,),
        in_specs=[pl.BlockSpec((tm, d), lambda i: (i, 0)),
                  pl.BlockSpec((1, d), lambda i: (0, 0))],
        out_specs=pl.BlockSpec((tm, d), lambda i: (i, 0)),
        out_shape=jax.ShapeDtypeStruct((m, d), out_dtype),
        compiler_params=_params("parallel"),
        name="rmsnorm",
    )(x, g.reshape(1, d))


def _prenorm_kernel(x_ref, g_ref, hu_ref, rs_ref):
    x = x_ref[...]
    ms = jnp.mean(x * x, axis=-1, keepdims=True)
    hu_ref[...] = (x * g_ref[...]).astype(hu_ref.dtype)
    rs_ref[...] = jnp.broadcast_to(lax.rsqrt(ms + EPS), rs_ref.shape)


def _prenorm(x, g, tm=256):
    m, d = x.shape
    return pl.pallas_call(
        _prenorm_kernel,
        grid=(m // tm,),
        in_specs=[pl.BlockSpec((tm, d), lambda i: (i, 0)),
                  pl.BlockSpec((1, d), lambda i: (0, 0))],
        out_specs=[pl.BlockSpec((tm, d), lambda i: (i, 0)),
                   pl.BlockSpec((tm, LANE), lambda i: (i, 0))],
        out_shape=[jax.ShapeDtypeStruct((m, d), BF16), jax.ShapeDtypeStruct((m, LANE), F32)],
        compiler_params=_params("parallel"),
        name="prenorm",
    )(x, g.reshape(1, d))


def _cast_on_first_m_step(pairs):
    @pl.when(pl.program_id(1) == 0)
    def _():
        for src, dst in pairs:
            dst[...] = src[...].astype(BF16)


def _ws_mm_kernel(a_ref, w_ref, rs_ref, o_ref, wb_ref):
    _cast_on_first_m_step([(w_ref, wb_ref)])
    acc = jnp.dot(a_ref[...], wb_ref[...], preferred_element_type=F32)
    o_ref[...] = (acc * rs_ref[:, :1]).astype(o_ref.dtype)


def _matmul_ws(a, w_stack, layer, row_scale, *, tm, tn, out_dtype, name):
    m, k = a.shape
    n = w_stack.shape[2]
    return pl.pallas_call(
        _ws_mm_kernel,
        grid=(n // tn, m // tm),
        in_specs=[pl.BlockSpec((tm, k), lambda j, i: (i, 0)),
                  pl.BlockSpec((None, k, tn), lambda j, i: (layer, 0, j)),
                  pl.BlockSpec((None, tm, LANE), lambda j, i: (row_scale.shape[0] - 1, i, 0))],
        out_specs=pl.BlockSpec((tm, tn), lambda j, i: (i, j)),
        out_shape=jax.ShapeDtypeStruct((m, n), out_dtype),
        scratch_shapes=[pltpu.VMEM((k, tn), BF16)],
        compiler_params=_params("arbitrary", "arbitrary"),
        name=name,
    )(a, w_stack, row_scale)


def _pw_fill(w_refs, slot_refs):
    for w_ref, slot_ref in zip(w_refs, slot_refs):
        rows = w_ref.shape[0]
        r0 = pl.multiple_of(pl.program_id(1) * rows, rows)
        slot_ref[pl.ds(r0, rows), :] = w_ref[...].astype(BF16)


def _pw_phases(w_refs, even_slots, odd_slots, compute, warmup=None):
    j = pl.program_id(0)

    @pl.when(j == 0)
    def _():
        _pw_fill(w_refs, even_slots)
        if warmup is not None:
            warmup()

    @pl.when((j > 0) & (j % 2 == 1))
    def _():
        _pw_fill(w_refs, odd_slots)
        compute(even_slots)

    @pl.when((j > 0) & (j % 2 == 0))
    def _():
        _pw_fill(w_refs, even_slots)
        compute(odd_slots)


def _pw_mm_kernel(a_ref, w_ref, rs_ref, o_ref, w0_ref, w1_ref):
    def compute(slots):
        acc = jnp.dot(a_ref[...], slots[0][...], preferred_element_type=F32)
        o_ref[...] = (acc * rs_ref[:, :1]).astype(o_ref.dtype)
    _pw_phases([w_ref], [w0_ref], [w1_ref], compute)


def _pw_mm_res_kernel(a_ref, w_ref, r_ref, o_ref, w0_ref, w1_ref):
    def compute(slots):
        o_ref[...] = r_ref[...] + jnp.dot(a_ref[...], slots[0][...], preferred_element_type=F32)
    _pw_phases([w_ref], [w0_ref], [w1_ref], compute)


def _pw_mm_res_norm_kernel(a_ref, w_ref, r_ref, g_ref, o_ref, hu_ref, rs_ref, w0_ref, w1_ref, ssq_ref, *, d):
    tm = o_ref.shape[0]
    tile_rows = pl.ds(pl.multiple_of(pl.program_id(1) * tm, tm), tm)

    def warmup():
        ssq_ref[tile_rows, :] = jnp.zeros((tm, LANE), F32)

    def compute(slots):
        x = r_ref[...] + jnp.dot(a_ref[...], slots[0][...], preferred_element_type=F32)
        o_ref[...] = x
        hu_ref[...] = (x * g_ref[...]).astype(hu_ref.dtype)
        ssq = ssq_ref[tile_rows, :] + jnp.sum(x * x, axis=-1, keepdims=True)
        ssq_ref[tile_rows, :] = ssq
        rs_ref[...] = lax.rsqrt(ssq * (1.0 / d) + EPS)
    _pw_phases([w_ref], [w0_ref], [w1_ref], compute, warmup)


def _pw_swiglu_kernel(h_ref, wg_ref, wu_ref, rs_ref, o_ref, g0_ref, u0_ref, g1_ref, u1_ref):
    def compute(slots):
        h = h_ref[...]
        rs = rs_ref[:, :1]
        a = jnp.dot(h, slots[0][...], preferred_element_type=F32) * rs
        b = jnp.dot(h, slots[1][...], preferred_element_type=F32) * rs
        o_ref[...] = (a * (1.0 / (1.0 + jnp.exp(-a))) * b).astype(o_ref.dtype)
    _pw_phases([wg_ref, wu_ref], [g0_ref, u0_ref], [g1_ref, u1_ref], compute)


def _pw_swiglu_into_kernel(h_ref, wg_ref, wu_ref, rs_ref, into_ref, o_ref, g0_ref, u0_ref, g1_ref, u1_ref):
    del into_ref
    _pw_swiglu_kernel(h_ref, wg_ref, wu_ref, rs_ref, o_ref, g0_ref, u0_ref, g1_ref, u1_ref)


def _pw_call(kern, a, w_stacks, layer, *, tm, tn, out_dtype, name, row_scale=None, residual=None,
             next_gain=None, tiles=None, into=None):
    m, k = a.shape
    n = w_stacks[0].shape[2]
    first, n_n = (0, n // tn) if tiles is None else tiles
    assert (first + n_n) * tn <= n and (tiles is not None or n % tn == 0), (n, tn, tiles)
    n_m = m // tm
    rows = k // n_m
    assert k % n_m == 0 and rows % 16 == 0, (k, n_m)
    m_idx = lambda j, i: jnp.where(j == 0, 0, i)
    n_idx = lambda j: first + jnp.maximum(j - 1, 0)
    w_spec = pl.BlockSpec((None, rows, tn), lambda j, i: (layer, jnp.where(j < n_n, i, n_m - 1),
                                                          first + jnp.minimum(j, n_n - 1)))
    o_spec = pl.BlockSpec((tm, tn), lambda j, i: (m_idx(j, i), n_idx(j)))
    in_specs = [pl.BlockSpec((tm, k), lambda j, i: (m_idx(j, i), 0))] + [w_spec] * len(w_stacks)
    args = [a, *w_stacks]
    out_specs = o_spec
    out_shape = jax.ShapeDtypeStruct((m, n), out_dtype)
    aliases = {}
    scratch = [pltpu.VMEM((k, tn), BF16)] * (2 * len(w_stacks))
    if row_scale is not None:
        in_specs.append(pl.BlockSpec((None, tm, LANE),
                                     lambda j, i: (row_scale.shape[0] - 1, m_idx(j, i), 0)))
        args.append(row_scale)
    if residual is not None:
        in_specs.append(o_spec)
        args.append(residual)
    if next_gain is not None:
        in_specs.append(pl.BlockSpec((1, tn), lambda j, i: (0, n_idx(j))))
        args.append(next_gain.reshape(1, n))
        out_specs = [o_spec, o_spec, pl.BlockSpec((None, tm, LANE), lambda j, i: (n_idx(j), m_idx(j, i), 0))]
        out_shape = [out_shape, jax.ShapeDtypeStruct((m, n), BF16),
                     jax.ShapeDtypeStruct((n_n, m, LANE), F32)]
        scratch.append(pltpu.VMEM((m, LANE), F32))
        kern = functools.partial(kern, d=n)
    if into is not None:
        assert next_gain is None and into.shape == (m, n) and into.dtype == out_dtype
        aliases = {len(args): 0}
        in_specs.append(pl.BlockSpec(memory_space=pl.ANY))
        args.append(into)
    return pl.pallas_call(
        kern,
        grid=(n_n + 1, n_m),
        in_specs=in_specs,
        out_specs=out_specs,
        out_shape=out_shape,
        scratch_shapes=scratch,
        input_output_aliases=aliases,
        compiler_params=_params("arbitrary", "arbitrary"),
        name=name,
    )(*args)


def _attn_group_start(g, rows):
    return min(max(g * ATTN_Q_ROWS - WIN_R // 2, 0), rows - ATTN_K_ROWS)


def _attn_row_windows(g, rows):
    start = _attn_group_start(g, rows)
    out = []
    for i in range(ATTN_Q_ROWS):
        qr = g * ATTN_Q_ROWS + i
        rs = min(max(qr - WIN_R // 2, 0), rows - WIN_R)
        out.append(((rs - start) * GRID_W, start - qr + (WIN_R - 1) + ATTN_SRC_LEFT))
    return tuple(out)


def _attn_group_types(rows):
    sigs, idx = [], []
    for g in range(rows // ATTN_Q_ROWS):
        sig = _attn_row_windows(g, rows)
        if sig not in sigs:
            sigs.append(sig)
        idx.append(sigs.index(sig))
    return sigs, idx


ATTN_SRC_LEFT = ATTN_Q_ROWS - 1
ATTN_SRC_W = -(-((WIN_R - 1 + ATTN_SRC_LEFT + ATTN_K_ROWS) * GRID_W) // LANE) * LANE


def _attn_bias_source(rpb):
    nl, h, n_ro, _ = rpb.shape
    rpb = rpb.astype(F32) * (HEAD_DIM ** 0.5)
    qc = jnp.arange(GRID_W)[:, None]
    kc = jnp.arange(GRID_W)[None, :]
    co = jnp.clip(kc - qc, -(WIN_C - 1), WIN_C - 1) + (WIN_C - 1)
    pick = (jnp.arange(2 * WIN_C - 1)[:, None, None] == co[None]).astype(F32)
    toep = jnp.einsum("lhrk,kqc->lhqrc", rpb, pick, precision=lax.Precision.HIGHEST)
    cs = jnp.clip(qc - WIN_C // 2, 0, GRID_W - WIN_C)
    valid = (kc >= cs) & (kc < cs + WIN_C)
    toep = jnp.where(valid[:, None, :], toep, MASKED)
    by_q = toep.reshape(nl, h, GRID_W, n_ro * GRID_W)
    copies = []
    for shift in (0, 1):
        left = (ATTN_SRC_LEFT - shift) * GRID_W
        pad = ((0, 0), (0, 0), (0, 0), (left, ATTN_SRC_W - left - n_ro * GRID_W))
        copies.append(jnp.pad(by_q, pad, constant_values=MASKED))
    return jnp.stack(copies, axis=2)


def _attn_fill_tables(src_ref, tab_ref, rows):
    nk = ATTN_K_ROWS * GRID_W
    lane = lax.broadcasted_iota(jnp.int32, (GRID_W, nk), 1)
    for t, sig in enumerate(_attn_group_types(rows)[0]):
        for i, (lo, u) in enumerate(sig):
            copy = u % 2
            window = src_ref[copy, :, pl.ds((u - copy) * GRID_W, nk)]
            valid = (lane >= lo) & (lane < lo + WIN_R * GRID_W)
            tab_ref[t, pl.ds(i * GRID_W, GRID_W), :] = jnp.where(valid, window, MASKED)


def _attn_kernel(q_ref, k_ref, v_ref, src_ref, o_ref, tab_ref, *, rows):
    n_groups = rows // ATTN_Q_ROWS
    tq = ATTN_Q_ROWS * GRID_W
    nk = ATTN_K_ROWS * GRID_W
    exp2_scale = (HEAD_DIM ** -0.5) * 1.4426950408889634
    _attn_fill_tables(src_ref, tab_ref, rows)
    group_type = _attn_group_types(rows)[1]
    for g in range(n_groups):
        start = _attn_group_start(g, rows) * GRID_W
        q = q_ref[pl.ds(g * tq, tq), :]
        kw = k_ref[pl.ds(start, nk), :]
        vw = v_ref[pl.ds(start, nk), :]
        s = lax.dot_general(q, kw, (((1,), (1,)), ((), ())), preferred_element_type=F32)
        s = s + tab_ref[group_type[g]]
        p = jnp.exp2((s - jnp.max(s, axis=-1, keepdims=True)) * exp2_scale)
        vw_ext = jnp.concatenate([vw, jnp.ones((nk, HEAD_DIM), BF16)], axis=1)
        o = jnp.dot(p.astype(BF16), vw_ext, preferred_element_type=F32)
        o_ref[pl.ds(g * tq, tq), :] = (o[:, :HEAD_DIM] / o[:, HEAD_DIM:HEAD_DIM + 1]).astype(o_ref.dtype)


def _attention(z, bias_src, layer, batch, seq):
    rows = seq // GRID_W
    tq = ATTN_Q_ROWS * GRID_W
    tk = ATTN_K_ROWS * GRID_W
    return pl.pallas_call(
        functools.partial(_attn_kernel, rows=rows),
        grid=(batch, N_HEADS),
        in_specs=[pl.BlockSpec((seq, HEAD_DIM), lambda b, h: (b, h)),
                  pl.BlockSpec((seq, HEAD_DIM), lambda b, h: (b, N_HEADS + h)),
                  pl.BlockSpec((seq, HEAD_DIM), lambda b, h: (b, 2 * N_HEADS + h)),
                  pl.BlockSpec((None, None, 2, GRID_W, ATTN_SRC_W), lambda b, h: (layer, h, 0, 0, 0))],
        out_specs=pl.BlockSpec((seq, HEAD_DIM), lambda b, h: (b, h)),
        out_shape=jax.ShapeDtypeStruct((batch * seq, ATTN_W), BF16),
        scratch_shapes=[pltpu.VMEM((len(_attn_group_types(rows)[0]), tq, tk), F32)],
        compiler_params=_params("parallel", "parallel"),
        name="nbr_attention",
    )(z, z, z, bias_src)


def _pool_band_matrices():
    i = jnp.arange(POOL_CHUNK)[:, None]
    off = jnp.arange(POOL_CHUNK + 2 * POOL_HALO)[None, :] - POOL_HALO - i
    return jnp.stack([((off >= -(w // 2)) & (off <= w // 2 - 1)).astype(BF16) for w in POOL_WINDOWS])


def _pool_kernel(p_ref, band_ref, w_ref, sc_ref, o_ref, pad_ref, *, seq):
    g = pl.program_id(1)
    half = jnp.left_shift(1, g)
    cols = p_ref.shape[1]
    pad_ref[pl.ds(0, POOL_HALO), :] = jnp.zeros((POOL_HALO, cols), BF16)
    pad_ref[pl.ds(POOL_HALO + seq, POOL_HALO), :] = jnp.zeros((POOL_HALO, cols), BF16)
    pad_ref[pl.ds(POOL_HALO, seq), :] = p_ref[...]

    for c in range(seq // POOL_CHUNK):
        base = c * POOL_CHUNK
        ph = pad_ref[pl.ds(base, POOL_CHUNK + 2 * POOL_HALO), :]
        wsum = jnp.dot(band_ref[0], ph, preferred_element_type=F32)
        t = base + lax.broadcasted_iota(jnp.int32, (POOL_CHUNK, 1), 0)
        lo = jnp.maximum(t - half, 0)
        hi = jnp.minimum(t + half - 1, seq - 1)
        cnt = (hi - lo + 1).astype(F32)
        centre = ph[POOL_HALO:POOL_HALO + POOL_CHUNK].astype(F32)
        d = wsum / cnt - centre
        y = jnp.dot(d.astype(BF16), w_ref[0], preferred_element_type=F32) * sc_ref[...]
        o_ref[pl.ds(base, POOL_CHUNK), :] = y.astype(o_ref.dtype)


def _pool(z, bands, w_grp, scale, batch, seq, col_block0):
    n_g = len(POOL_WINDOWS)
    return pl.pallas_call(
        functools.partial(_pool_kernel, seq=seq),
        grid=(batch, n_g),
        in_specs=[pl.BlockSpec((seq, POOL_GROUP), lambda b, g: (b, col_block0 + g)),
                  pl.BlockSpec((1, POOL_CHUNK, POOL_CHUNK + 2 * POOL_HALO), lambda b, g: (g, 0, 0)),
                  pl.BlockSpec((1, POOL_GROUP, POOL_GROUP), lambda b, g: (g, 0, 0)),
                  pl.BlockSpec((1, POOL_GROUP), lambda b, g: (0, g))],
        out_specs=pl.BlockSpec((seq, POOL_GROUP), lambda b, g: (b, g)),
        out_shape=jax.ShapeDtypeStruct((batch * seq, POOL_W), BF16),
        scratch_shapes=[pltpu.VMEM((seq + 2 * POOL_HALO, POOL_GROUP), BF16)],
        compiler_params=_params("parallel", "arbitrary"),
        name="multiscale_pool",
    )(z, bands, w_grp, scale.reshape(1, POOL_W))


def _gelu_tanh(x):
    return 0.5 * x * (1.0 + jnp.tanh(0.7978845608028654 * (x + 0.044715 * (x * x * x))))


def _gmlp_kernel(u_ref, v_ref, lng_ref, lnb_ref, ws_ref, bs_ref, o_ref):
    v = _gelu_tanh(v_ref[...].astype(F32))
    mu = jnp.mean(v, axis=-1, keepdims=True)
    vc = v - mu
    var = jnp.mean(vc * vc, axis=-1, keepdims=True)
    vn = (vc * lax.rsqrt(var + EPS) * lng_ref[...] + lnb_ref[...]).astype(BF16)
    n_chunks = u_ref.shape[0] // GMLP_CHUNK
    for c in range(n_chunks):
        r = slice(c * GMLP_CHUNK, (c + 1) * GMLP_CHUNK)
        for g in range(GMLP_GROUPS):
            cs = slice(g * GMLP_GROUP_W, (g + 1) * GMLP_GROUP_W)
            mixed = jnp.dot(ws_ref[g], vn[r, cs], preferred_element_type=F32) + bs_ref[g]
            u = _gelu_tanh(u_ref[r, cs].astype(F32))
            o_ref[r, cs] = (u * mixed).astype(o_ref.dtype)


def _gmlp(z, ln_g, ln_b, w_s, b_s, u_block, tm=512):
    m = z.shape[0]
    return pl.pallas_call(
        _gmlp_kernel,
        grid=(m // tm,),
        in_specs=[pl.BlockSpec((tm, GMLP_W), lambda i: (i, u_block)),
                  pl.BlockSpec((tm, GMLP_W), lambda i: (i, u_block + 1)),
                  pl.BlockSpec((1, GMLP_W), lambda i: (0, 0)),
                  pl.BlockSpec((1, GMLP_W), lambda i: (0, 0)),
                  pl.BlockSpec((GMLP_GROUPS, GMLP_CHUNK, GMLP_CHUNK), lambda i: (0, 0, 0)),
                  pl.BlockSpec((GMLP_GROUPS, GMLP_CHUNK, 1), lambda i: (0, 0, 0))],
        out_specs=pl.BlockSpec((tm, GMLP_W), lambda i: (i, 0)),
        out_shape=jax.ShapeDtypeStruct((m, GMLP_W), BF16),
        compiler_params=_params("parallel"),
        name="spatial_gating",
    )(z, z, ln_g.reshape(1, GMLP_W), ln_b.reshape(1, GMLP_W), w_s, b_s.reshape(GMLP_GROUPS, GMLP_CHUNK, 1))


def _sigmoid(x):
    return 1.0 / (1.0 + jnp.exp(-x))


def _merge_kernel(ya_ref, yp_ref, ys_ref, hd_ref, wb_ref, gu0_ref, gu1_ref, gu2_ref,
                  gb0_ref, gb1_ref, gb2_ref, o_ref, wb0_ref, gus0_ref, wb1_ref, gus1_ref):
    def compute(slots):
        wbb_ref, gua_ref, gup_ref, gus_ref = slots
        hd = hd_ref[...]
        o_pool = ATTN_W
        o_sg = ATTN_W + POOL_W
        ga = _sigmoid(jnp.dot(hd, gua_ref[...], preferred_element_type=F32) + gb0_ref[...])
        acc = ga * jnp.dot(ya_ref[...], wbb_ref[pl.ds(0, ATTN_W), :], preferred_element_type=F32)
        gp = _sigmoid(jnp.dot(hd, gup_ref[...], preferred_element_type=F32) + gb1_ref[...])
        acc = acc + gp * jnp.dot(yp_ref[...], wbb_ref[pl.ds(o_pool, POOL_W), :], preferred_element_type=F32)
        gs = _sigmoid(jnp.dot(hd, gus_ref[...], preferred_element_type=F32) + gb2_ref[...])
        acc = acc + gs * jnp.dot(ys_ref[...], wbb_ref[pl.ds(o_sg, GMLP_W), :], preferred_element_type=F32)
        o_ref[...] = acc.astype(o_ref.dtype)

    def slots(wbb_ref, gub_ref):
        return [wbb_ref, gub_ref.at[0], gub_ref.at[1], gub_ref.at[2]]

    _pw_phases([wb_ref, gu0_ref, gu1_ref, gu2_ref], slots(wb0_ref, gus0_ref), slots(wb1_ref, gus1_ref), compute)


def _merge(ya, yp, ys, hd, wb_stack, gate_up_stack, gate_b, layer, *, tm, tn):
    m = ya.shape[0]
    mix_w, d = wb_stack.shape[1:]
    rank = hd.shape[1]
    n_n, n_m = d // tn, m // tm
    assert mix_w % n_m == 0 and rank % n_m == 0 and (rank // n_m) % 16 == 0, (mix_w, rank, n_m)
    m_idx = lambda j, i: jnp.where(j == 0, 0, i)
    n_idx = lambda j: jnp.maximum(j - 1, 0)
    chunk = lambda j, i: jnp.where(j < n_n, i, n_m - 1)
    tile = lambda j: jnp.minimum(j, n_n - 1)
    row = lambda w: pl.BlockSpec((tm, w), lambda j, i: (m_idx(j, i), 0))
    gate_b = gate_b.reshape(1, N_BRANCH * d)
    gu = lambda br: pl.BlockSpec((None, rank // n_m, tn),
                                 lambda j, i: (layer, chunk(j, i), br * n_n + tile(j)))
    gb = lambda br: pl.BlockSpec((1, tn), lambda j, i: (0, br * n_n + n_idx(j)))
    slot_shapes = [pltpu.VMEM((mix_w, tn), BF16), pltpu.VMEM((N_BRANCH, rank, tn), BF16)]
    return pl.pallas_call(
        _merge_kernel,
        grid=(n_n + 1, n_m),
        in_specs=[row(ATTN_W), row(POOL_W), row(GMLP_W), row(rank),
                  pl.BlockSpec((None, mix_w // n_m, tn), lambda j, i: (layer, chunk(j, i), tile(j))),
                  gu(0), gu(1), gu(2), gb(0), gb(1), gb(2)],
        out_specs=pl.BlockSpec((tm, tn), lambda j, i: (m_idx(j, i), n_idx(j))),
        out_shape=jax.ShapeDtypeStruct((m, d), BF16),
        scratch_shapes=slot_shapes + slot_shapes,
        compiler_params=_params("arbitrary", "arbitrary"),
        name="branch_merge",
    )(ya, yp, ys, hd, wb_stack, gate_up_stack, gate_up_stack, gate_up_stack, gate_b, gate_b, gate_b)


def kernel(x, attn_norm_g, w_in, rpb, pool_w, pool_scale, gmlp_ln_g, gmlp_ln_b, gmlp_w_s, gmlp_b_s,
           w_branch, gate_down, gate_up, gate_b, w_out, ffn_norm_g, w_ffn_gate, w_ffn_up, w_ffn_down,
           final_norm_g):
    batch, seq, d = x.shape
    depth = w_in.shape[0]
    d_ff = w_ffn_gate.shape[2]
    xf = x.reshape(batch * seq, d)
    bands = _pool_band_matrices()
    pool_col_block = (3 * ATTN_W) // POOL_GROUP
    u_block = (3 * ATTN_W + POOL_W) // GMLP_W
    bias_src = _attn_bias_source(rpb)
    hu, rs = _prenorm(xf, attn_norm_g[0])
    rs = rs[None]
    for l in range(depth):
        z = _pw_call(_pw_mm_kernel, hu, [w_in], l, tm=1024, tn=1024, out_dtype=BF16, row_scale=rs,
                     name="in_proj")
        hd = _matmul_ws(hu, gate_down, l, rs, tm=1024, tn=gate_down.shape[2], out_dtype=BF16,
                        name="gate_down")
        y_attn = _attention(z, bias_src, l, batch, seq)
        y_pool = _pool(z, bands, pool_w[l].astype(BF16), pool_scale[l], batch, seq, pool_col_block)
        y_sg = _gmlp(z, gmlp_ln_g[l], gmlp_ln_b[l], gmlp_w_s[l].astype(BF16), gmlp_b_s[l], u_block)
        merged = _merge(y_attn, y_pool, y_sg, hd, w_branch, gate_up, gate_b[l], l, tm=1024, tn=1024)
        xf, hu, rs = _pw_call(_pw_mm_res_norm_kernel, merged, [w_out], l, tm=512, tn=1024, out_dtype=F32,
                              residual=xf, next_gain=ffn_norm_g[l], name="out_proj")
        n_wide = d_ff // SWIGLU_WIDE_TN
        act = _pw_call(_pw_swiglu_kernel, hu, [w_ffn_gate, w_ffn_up], l, tm=1024, tn=SWIGLU_WIDE_TN,
                       out_dtype=BF16, row_scale=rs, tiles=(0, n_wide), name="swiglu")
        rest = d_ff - n_wide * SWIGLU_WIDE_TN
        if rest:
            tail_tiles = (n_wide * SWIGLU_WIDE_TN // SWIGLU_TAIL_TN, rest // SWIGLU_TAIL_TN)
            act = _pw_call(_pw_swiglu_into_kernel, hu, [w_ffn_gate, w_ffn_up], l, tm=2048, tn=SWIGLU_TAIL_TN,
                           out_dtype=BF16, row_scale=rs, tiles=tail_tiles, into=act, name="swiglu_tail")
        if l + 1 < depth:
            xf, hu, rs = _pw_call(_pw_mm_res_norm_kernel, act, [w_ffn_down], l, tm=512, tn=512,
                                  out_dtype=F32, residual=xf, next_gain=attn_norm_g[l + 1],
                                  name="ffn_down")
        else:
            xf = _pw_call(_pw_mm_res_kernel, act, [w_ffn_down], l, tm=512, tn=512, out_dtype=F32,
                          residual=xf, name="ffn_down")
    out = _rmsnorm(xf, final_norm_g, F32)
    return out.reshape(batch, seq, d)
```
